```python
import math
import jax, jax.numpy as jnp
from jax import lax
import numpy as np

D_MODEL = 1024
BATCH = 4
SEQ = 8192
DEPTH = 2

D_MIX = D_MODEL
HEAD_DIM = 64
M_WIDTH = D_MIX // 2
M_HEAD_DIM = 128
M_HEADS = M_WIDTH // M_HEAD_DIM
M_CHUNK = 64
CONV_W = 4
B_WIDTH = D_MIX // 4
B_HEADS = B_WIDTH // HEAD_DIM
MOBA_BLOCK = 256
MOBA_TOPK = 3
N_WIDTH = D_MIX // 4
N_HEADS = N_WIDTH // HEAD_DIM
CMP_LEN = 32
CMP_STRIDE = 16
CMP_HIDDEN = 128
SEL_BLOCK = 64
SEL_TOPK = 16
WINDOW = 512
Q_BLOCK = 64
D_FF = 4 * D_MODEL
ROPE_THETA = 500000.0
ROT_DIM = HEAD_DIM // 4
NORM_EPS = 1e-6
NEG = -1e30
BIG = 1e9

SPLITS = ((M_WIDTH,) * 4 + (M_HEADS, M_HEADS)
          + (B_WIDTH,) * 3
          + (N_WIDTH,) + (HEAD_DIM,) * 6
          + (3 * N_HEADS,))
D_IN = sum(SPLITS)
SPLIT_POINTS = tuple(int(v) for v in np.cumsum(SPLITS)[:-1])

kernel_name = "hybrid_mlstm_moba_nsa_block"


def rmsnorm(x, g):
    xf = x.astype(jnp.float32)
    y = xf * lax.rsqrt(jnp.mean(xf * xf, axis=-1, keepdims=True) + NORM_EPS)
    return (y * g.astype(jnp.float32)).astype(x.dtype)


def partial_rope(x):
    S = x.shape[-2]
    half = ROT_DIM // 2
    inv_freq = jnp.exp(-math.log(ROPE_THETA) * jnp.arange(half, dtype=jnp.float32) * (2.0 / ROT_DIM))
    ang = jnp.arange(S, dtype=jnp.float32)[:, None] * inv_freq[None, :]
    cos, sin = jnp.cos(ang), jnp.sin(ang)
    x1, x2, rest = x[..., :half], x[..., half:ROT_DIM], x[..., ROT_DIM:]
    return jnp.concatenate([x1 * cos - x2 * sin, x2 * cos + x1 * sin, rest], axis=-1)


def causal_conv(x, w):
    S = x.shape[1]
    xp = jnp.pad(x, ((0, 0), (CONV_W - 1, 0), (0, 0)))
    return sum(xp[:, j:j + S] * w[j] for j in range(CONV_W))


def masked_softmax(s, mask):
    p = jax.nn.softmax(jnp.where(mask, s, NEG), axis=-1)
    return jnp.where(mask, p, 0.0)


def mlstm(q, k, v, o_pre, i_pre, f_pre, norm_g):
    B, S, _ = q.shape
    NC, L, H, D = S // M_CHUNK, M_CHUNK, M_HEADS, M_HEAD_DIM

    def to_chunks(t):
        return t.reshape(B, NC, L, H, D).transpose(1, 0, 3, 2, 4)

    qc, kc, vc = to_chunks(q), to_chunks(k) * (D ** -0.5), to_chunks(v)
    lf = jax.nn.log_sigmoid(f_pre).reshape(B, NC, L, H).transpose(1, 0, 3, 2)
    li = i_pre.reshape(B, NC, L, H).transpose(1, 0, 3, 2)
    causal = jnp.tril(jnp.ones((L, L), dtype=bool))

    def step(carry, xs):
        C, n, m = carry
        qt, kt, vt, lft, lit = xs
        b = jnp.cumsum(lft, axis=-1)
        dmat = jnp.where(causal, b[..., :, None] - b[..., None, :] + lit[..., None, :], -jnp.inf)
        inter = b + m[..., None]
        m_t = jnp.maximum(inter, jnp.max(dmat, axis=-1))
        w_intra = jnp.exp(dmat - m_t[..., None])
        w_prev = jnp.exp(inter - m_t)
        s = jnp.einsum('bhtd,bhsd->bhts', qt, kt) * w_intra
        num = jnp.einsum('bhts,bhse->bhte', s, vt) + w_prev[..., None] * jnp.einsum('bhtd,bhde->bhte', qt, C)
        den = jnp.sum(s, axis=-1) + w_prev * jnp.einsum('bhtd,bhd->bht', qt, n)
        h = num / jnp.maximum(jnp.abs(den), jnp.exp(-m_t))[..., None]
        b_last = b[..., -1]
        g = b_last[..., None] - b + lit
        m_new = jnp.maximum(b_last + m, jnp.max(g, axis=-1))
        a = jnp.exp(b_last + m - m_new)
        w_s = jnp.exp(g - m_new[..., None])
        C = a[..., None, None] * C + jnp.einsum('bhs,bhsd,bhse->bhde', w_s, kt, vt)
        n = a[..., None] * n + jnp.einsum('bhs,bhsd->bhd', w_s, kt)
        return (C, n, m_new), h

    init = (jnp.zeros((B, H, D, D), jnp.float32), jnp.zeros((B, H, D), jnp.float32),
            jnp.zeros((B, H), jnp.float32))
    _, h = lax.scan(step, init, (qc, kc, vc, lf, li))
    h = h.transpose(1, 0, 3, 2, 4).reshape(B, S, H, D)
    h = rmsnorm(h, norm_g.reshape(H, D)).reshape(B, S, M_WIDTH)
    return jax.nn.sigmoid(o_pre) * h


def moba(q, k, v, q_g, k_g):
    B, S, _ = q.shape
    H, D, BS = B_HEADS, HEAD_DIM, MOBA_BLOCK
    heads = lambda t: t.reshape(B, S, H, D).transpose(0, 2, 1, 3)
    q = partial_rope(rmsnorm(heads(q), q_g))
    k = partial_rope(rmsnorm(heads(k), k_g))
    v = heads(v)
    NB = -(-S // BS)
    pad = NB * BS - S
    kp = jnp.pad(k, ((0, 0), (0, 0), (0, pad), (0, 0)))
    vp = jnp.pad(v, ((0, 0), (0, 0), (0, pad), (0, 0)))
    kb = kp.reshape(B, H, NB, BS, D)
    vb = vp.reshape(B, H, NB, BS, D)
    kmean = jnp.mean(kb, axis=3)
    topk = min(MOBA_TOPK, NB)
    scale = D ** -0.5
    gather = jax.vmap(jax.vmap(lambda blocks, idx: blocks[idx]))

    def block(j):
        t0 = j * Q_BLOCK
        tq = t0 + jnp.arange(Q_BLOCK)
        qj = lax.dynamic_slice_in_dim(q, t0, Q_BLOCK, axis=2)
        own = t0 // BS
        gs = jnp.einsum('bhqd,bhnd->bhqn', qj, kmean)
        gs = jnp.where(jnp.arange(NB) < own, gs, NEG)
        _, idx = lax.top_k(gs, topk)
        gk, gv = gather(kb, idx), gather(vb, idx)
        s_sel = jnp.einsum('bhqd,bhqrkd->bhqrk', qj, gk) * scale
        s_sel = jnp.where((jnp.arange(topk) < own)[:, None], s_sel, NEG).reshape(B, H, Q_BLOCK, topk * BS)
        ko = lax.dynamic_slice_in_dim(kp, own * BS, BS, axis=2)
        vo = lax.dynamic_slice_in_dim(vp, own * BS, BS, axis=2)
        s_own = jnp.einsum('bhqd,bhkd->bhqk', qj, ko) * scale
        s_own = jnp.where(own * BS + jnp.arange(BS)[None, :] <= tq[:, None], s_own, NEG)
        p = jax.nn.softmax(jnp.concatenate([s_sel, s_own], axis=-1), axis=-1)
        p_sel = p[..., :topk * BS].reshape(B, H, Q_BLOCK, topk, BS)
        p_own = p[..., topk * BS:]
        return (jnp.einsum('bhqrk,bhqrkd->bhqd', p_sel, gv)
                + jnp.einsum('bhqk,bhkd->bhqd', p_own, vo))

    out = lax.map(block, jnp.arange(S // Q_BLOCK))
    return out.transpose(1, 0, 3, 2, 4).reshape(B, S, B_WIDTH)


def nsa(q, kc, vc, ks, vs, kw, vw, g_pre, q_g, k_g, pe, w1, w2):
    B, S, _ = q.shape
    H, D = N_HEADS, HEAD_DIM
    scale = D ** -0.5
    q = rmsnorm(q.reshape(B, S, H, D), q_g).transpose(0, 2, 1, 3)
    qr = partial_rope(q)
    ks = partial_rope(rmsnorm(ks, k_g[1]))
    kw = partial_rope(rmsnorm(kw, k_g[2]))

    n_sub = CMP_LEN // CMP_STRIDE
    Nc = S // CMP_STRIDE - n_sub + 1

    def compress(t, pe_, w1_, w2_):
        c = t.reshape(B, S // CMP_STRIDE, CMP_STRIDE, D)
        blocks = jnp.concatenate([c[:, r:Nc + r] for r in range(n_sub)], axis=2) + pe_
        hid = jax.nn.silu(blocks.reshape(B, Nc, CMP_LEN * D) @ w1_)
        return hid @ w2_

    Kc = rmsnorm(compress(kc, pe[0], w1[0], w2[0]), k_g[0])
    Vc = compress(vc, pe[1], w1[1], w2[1])
    c_end = jnp.arange(Nc) * CMP_STRIDE + CMP_LEN - 1

    Nsel = S // SEL_BLOCK
    c_start = np.arange(Nc) * CMP_STRIDE
    s_start = np.arange(Nsel) * SEL_BLOCK
    overlap = jnp.asarray(((c_start[:, None] < s_start[None, :] + SEL_BLOCK)
                           & (c_start[:, None] + CMP_LEN > s_start[None, :])).astype(np.float32))
    ksb = ks.reshape(B, Nsel, SEL_BLOCK, D)
    vsb = vs.reshape(B, Nsel, SEL_BLOCK, D)
    ksel = min(SEL_TOPK, Nsel)
    gather_b = jax.vmap(lambda blocks, idx: blocks[idx])
    kw_pad = jnp.pad(kw, ((0, 0), (WINDOW, 0), (0, 0)))
    vw_pad = jnp.pad(vw, ((0, 0), (WINDOW, 0), (0, 0)))
    gates = jax.nn.sigmoid(g_pre).reshape(B, S, 3, H)

    def block(j):
        t0 = j * Q_BLOCK
        tq = t0 + jnp.arange(Q_BLOCK)
        qj = lax.dynamic_slice_in_dim(q, t0, Q_BLOCK, axis=2)
        qrj = lax.dynamic_slice_in_dim(qr, t0, Q_BLOCK, axis=2)
        s_c = jnp.einsum('bhqd,bcd->bhqc', qj, Kc) * scale
        p_c = masked_softmax(s_c, c_end[None, :] <= tq[:, None])
        o_c = jnp.einsum('bhqc,bcd->bhqd', p_c, Vc)
        imp = jnp.einsum('bhqc,cn->bqn', p_c, overlap)
        blk_q = tq // SEL_BLOCK
        jn = jnp.arange(Nsel)
        causal_blk = jn[None, :] <= blk_q[:, None]
        forced = causal_blk & ((jn[None, :] == 0) | (jn[None, :] >= blk_q[:, None] - 1))
        imp = jnp.where(forced, BIG, jnp.where(causal_blk, imp, NEG))
        _, sidx = lax.top_k(imp, ksel)
        gk, gv = gather_b(ksb, sidx), gather_b(vsb, sidx)
        pos = sidx[..., None] * SEL_BLOCK + jnp.arange(SEL_BLOCK)
        mask_s = (pos <= tq[None, :, None, None])[:, None]
        s_s = jnp.where(mask_s, jnp.einsum('bhqd,bqrkd->bhqrk', qrj, gk) * scale, NEG)
        p_s = jax.nn.softmax(s_s.reshape(B, H, Q_BLOCK, ksel * SEL_BLOCK), axis=-1)
        o_s = jnp.einsum('bhqrk,bqrkd->bhqd', p_s.reshape(B, H, Q_BLOCK, ksel, SEL_BLOCK), gv)
        kwj = lax.dynamic_slice_in_dim(kw_pad, t0, WINDOW + Q_BLOCK, axis=1)
        vwj = lax.dynamic_slice_in_dim(vw_pad, t0, WINDOW + Q_BLOCK, axis=1)
        posw = t0 - WINDOW + jnp.arange(WINDOW + Q_BLOCK)
        mask_w = ((posw[None, :] <= tq[:, None]) & (posw[None, :] > tq[:, None] - WINDOW)
                  & (posw[None, :] >= 0))
        s_w = jnp.where(mask_w, jnp.einsum('bhqd,bkd->bhqk', qrj, kwj) * scale, NEG)
        o_w = jnp.einsum('bhqk,bkd->bhqd', jax.nn.softmax(s_w, axis=-1), vwj)
        gj = lax.dynamic_slice_in_dim(gates, t0, Q_BLOCK, axis=1).transpose(0, 2, 3, 1)[..., None]
        return gj[:, 0] * o_c + gj[:, 1] * o_s + gj[:, 2] * o_w

    out = lax.map(block, jnp.arange(S // Q_BLOCK))
    return out.transpose(1, 0, 3, 2, 4).reshape(B, S, N_WIDTH)


def setup_inputs(seed: int = 0) -> dict:
    key = jax.random.key(seed)
    ks = jax.random.split(key, 20)
    nrm = lambda k, shape, s: jax.random.normal(k, shape, jnp.float32) * s
    L = DEPTH
    b_i = nrm(ks[2], (L, M_HEADS), 0.1)
    b_f = jnp.linspace(3.0, 6.0, M_HEADS, dtype=jnp.float32)[None, :] + nrm(ks[3], (L, M_HEADS), 0.1)
    return {
        "x": nrm(ks[0], (BATCH, SEQ, D_MODEL), 1.0),
        "w_in": nrm(ks[1], (L, D_MODEL, D_IN), D_MODEL ** -0.5),
        "b_if": jnp.concatenate([b_i, b_f], axis=-1),
        "conv_qk": nrm(ks[4], (L, CONV_W, 2 * M_WIDTH), CONV_W ** -0.5),
        "m_norm": 1.0 + nrm(ks[5], (L, M_WIDTH), 0.02),
        "moba_qk_norm": 1.0 + nrm(ks[6], (L, 2, HEAD_DIM), 0.02),
        "nsa_q_norm": 1.0 + nrm(ks[7], (L, HEAD_DIM), 0.02),
        "nsa_k_norm": 1.0 + nrm(ks[8], (L, 3, HEAD_DIM), 0.02),
        "cmp_pe": nrm(ks[9], (L, 2, CMP_LEN, HEAD_DIM), 0.02),
        "cmp_w1": nrm(ks[10], (L, 2, CMP_LEN * HEAD_DIM, CMP_HIDDEN), (CMP_LEN * HEAD_DIM) ** -0.5),
        "cmp_w2": nrm(ks[11], (L, 2, CMP_HIDDEN, HEAD_DIM), CMP_HIDDEN ** -0.5),
        "w_out": nrm(ks[12], (L, D_MIX, D_MODEL), D_MIX ** -0.5),
        "norm_mix": 1.0 + nrm(ks[13], (L, D_MODEL), 0.02),
        "norm_ffn": 1.0 + nrm(ks[14], (L, D_MODEL), 0.02),
        "w_ff1": nrm(ks[15], (L, D_MODEL, D_FF), D_MODEL ** -0.5),
        "w_ff2": nrm(ks[16], (L, D_FF, D_MODEL), D_FF ** -0.5),
    }


def reference(x, w_in, b_if, conv_qk, m_norm, moba_qk_norm, nsa_q_norm, nsa_k_norm,
              cmp_pe, cmp_w1, cmp_w2, w_out, norm_mix, norm_ffn, w_ff1, w_ff2):
    for l in range(DEPTH):
        h = rmsnorm(x, norm_mix[l])
        proj = jnp.einsum('bsd,de->bse', h, w_in[l]).astype(jnp.float32)
        (m_q, m_k, m_v, m_o, m_i, m_f, b_q, b_k, b_v,
         n_q, n_kc, n_vc, n_ks, n_vs, n_kw, n_vw, n_g) = jnp.split(proj, SPLIT_POINTS, axis=-1)
        qk = jax.nn.silu(causal_conv(jnp.concatenate([m_q, m_k], axis=-1), conv_qk[l]))
        gif = jnp.concatenate([m_i, m_f], axis=-1) + b_if[l]
        y_m = mlstm(qk[..., :M_WIDTH], qk[..., M_WIDTH:], m_v, m_o,
                    gif[..., :M_HEADS], gif[..., M_HEADS:], m_norm[l])
        y_b = moba(b_q, b_k, b_v, moba_qk_norm[l, 0], moba_qk_norm[l, 1])
        y_n = nsa(n_q, n_kc, n_vc, n_ks, n_vs, n_kw, n_vw, n_g, nsa_q_norm[l], nsa_k_norm[l],
                  cmp_pe[l], cmp_w1[l], cmp_w2[l])
        mix = jnp.concatenate([y_m, y_b, y_n], axis=-1).astype(x.dtype)
        x = x + jnp.einsum('bse,ed->bsd', mix, w_out[l])
        h = rmsnorm(x, norm_ffn[l])
        u = jnp.square(jax.nn.relu(jnp.einsum('bsd,df->bsf', h, w_ff1[l])))
        x = x + jnp.einsum('bsf,fd->bsd', u, w_ff2[l])
    return x
```

```python
import functools
import math

import jax
import jax.numpy as jnp
import numpy as np
from jax import lax
from jax.experimental import pallas as pl
from jax.experimental.pallas import tpu as pltpu

F32 = jnp.float32
BF16 = jnp.bfloat16

HEAD_DIM = 64
M_HEADS = 4
M_HEAD_DIM = 128
M_WIDTH = M_HEADS * M_HEAD_DIM
CONV_W = 4
ATT_HEADS = 4
ATT_WIDTH = ATT_HEADS * HEAD_DIM
MOBA_BLOCK = 256
MOBA_TOPK = 3
CMP_LEN = 32
CMP_STRIDE = 16
CMP_HIDDEN = 128
SEL_BLOCK = 64
SEL_TOPK = 16
WINDOW = 512
ROPE_THETA = 500000.0
ROT_DIM = HEAD_DIM // 4
ROT_HALF = ROT_DIM // 2
NORM_EPS = 1e-6
NEG = -1e30
BIG = 1e9
SCALE = HEAD_DIM ** -0.5

LANES = 128
AUG_W = 2 * LANES
VMEM_LIMIT = 56 * 1024 * 1024

COL_MQK, COL_MV, COL_MO = 0, 1024, 1536
COL_BQ, COL_BK, COL_BV = 2048, 2304, 2560
COL_NQ, COL_NKV, COL_SM = 2816, 3072, 3456
P_W = 3584
SM_I, SM_F, SM_G = 0, 4, 8

_NT = (((1,), (1,)), ((), ()))
_HI = lax.Precision.HIGHEST


def _iota(shape, dim):
    return lax.broadcasted_iota(jnp.int32, shape, dim)


def _cparams(n_axes):
    return pltpu.CompilerParams(dimension_semantics=("arbitrary",) * n_axes,
                                vmem_limit_bytes=VMEM_LIMIT)


def _const_spec(shape):
    nd = len(shape)
    return pl.BlockSpec(shape, lambda *_: (0,) * nd)


def _rms_groups(x, width):
    T, W = x.shape
    x2 = x * x
    lane = _iota((T, W), 1)
    scale = None
    for h in range(W // width):
        r = lax.rsqrt(jnp.mean(x2[:, h * width:(h + 1) * width], axis=-1, keepdims=True) + NORM_EPS)
        scale = r if scale is None else jnp.where(lane >= h * width, r, scale)
    return x * scale


def _rope(x, cos, sin):
    W = x.shape[1]
    lane = _iota(x.shape, 1) % HEAD_DIM
    up = pltpu.roll(x, W - ROT_HALF, 1)
    dn = pltpu.roll(x, ROT_HALF, 1)
    return x * cos + jnp.where(lane < ROT_HALF, up, dn) * sin


def _head_piece(x, h):
    pair = x[:, (h // 2) * LANES:(h // 2 + 1) * LANES]
    if h % 2:
        pair = pltpu.roll(pair, HEAD_DIM, 1)
    return jnp.where(_iota(pair.shape, 1) < HEAD_DIM, pair, 0.0)


def _top_k_mask(vals, k, rank_limit=None):
    lane = _iota(vals.shape, 1)
    sel = jnp.zeros(vals.shape, jnp.int32)
    for r in range(k):
        mx = jnp.max(vals, axis=-1, keepdims=True)
        idx = jnp.min(jnp.where(vals == mx, lane, LANES), axis=-1, keepdims=True)
        pick = lane == idx
        mark = 1 if rank_limit is None else jnp.where(rank_limit > r, 1, 0)
        sel = jnp.where(pick, mark, sel)
        vals = jnp.where(pick, -jnp.inf, vals)
    return sel


def _proj_kernel(x_ref, g_ref, w_ref, o_ref):
    x = x_ref[...]
    h = x * lax.rsqrt(jnp.mean(x * x, axis=-1, keepdims=True) + NORM_EPS) * g_ref[...]
    o_ref[...] = jnp.dot(h.astype(BF16), w_ref[...], preferred_element_type=F32)


def _proj(x2d, g, w, tm=256):
    M, D = x2d.shape
    N = w.shape[1]
    return pl.pallas_call(
        _proj_kernel,
        grid=(M // tm,),
        in_specs=[pl.BlockSpec((tm, D), lambda i: (i, 0)), _const_spec((1, D)), _const_spec((D, N))],
        out_specs=pl.BlockSpec((tm, N), lambda i: (i, 0)),
        out_shape=jax.ShapeDtypeStruct((M, N), F32),
        compiler_params=_cparams(1),
        name="in_proj",
    )(x2d, g, w)


def _mlstm_kernel(qk_ref, v_ref, o_ref, sm_ref, cw_ref, b_ref, g_ref, out_ref, xbuf, c_scr, m_scr, *, Lc):
    W2 = 2 * M_WIDTH
    D = M_HEAD_DIM

    @pl.when(pl.program_id(1) == 0)
    def _():
        xbuf[0:8, :] = jnp.zeros((8, W2), F32)
        c_scr[...] = jnp.zeros_like(c_scr)
        m_scr[...] = jnp.zeros_like(m_scr)

    x = qk_ref[0]
    xbuf[8:8 + Lc, :] = x
    y = cw_ref[0:1, :] * xbuf[8 - CONV_W + 1:8 - CONV_W + 1 + Lc, :]
    for j in range(1, CONV_W):
        y = y + cw_ref[j:j + 1, :] * xbuf[8 - CONV_W + 1 + j:8 - CONV_W + 1 + j + Lc, :]
    xbuf[0:8, :] = x[Lc - 8:Lc, :]
    qk = y * jax.nn.sigmoid(y)

    gi = sm_ref[0] + b_ref[...]
    lane = _iota((Lc, LANES), 1)
    lsig = jnp.minimum(gi, 0.0) - jnp.log1p(jnp.exp(-jnp.abs(gi)))
    gates = jnp.where(lane < SM_F, gi, jnp.where(lane < SM_F + M_HEADS, lsig, 0.0))
    gates_t = gates.T
    ri, ci = _iota((Lc, Lc), 0), _iota((Lc, Lc), 1)
    causal = ci <= ri
    b_col = jnp.dot(causal.astype(F32), gates, precision=_HI, preferred_element_type=F32)
    b_row = jnp.dot(gates_t[0:8, :], (ri <= ci).astype(F32), precision=_HI, preferred_element_type=F32)
    ones_col = jnp.where(lane == 0, 1.0, 0.0)

    for h in range(M_HEADS):
        q = qk[:, h * D:(h + 1) * D]
        k = qk[:, M_WIDTH + h * D:M_WIDTH + (h + 1) * D] * (D ** -0.5)
        v_aug = jnp.concatenate([v_ref[0, :, h * D:(h + 1) * D], ones_col], axis=1).astype(BF16)
        bc = b_col[:, SM_F + h:SM_F + h + 1]
        br = b_row[SM_F + h:SM_F + h + 1, :]
        li_r = gates_t[SM_I + h:SM_I + h + 1, :]
        li_c = gates[:, SM_I + h:SM_I + h + 1]
        m_prev = m_scr[h][0:1, 0:1]

        dmat = jnp.where(causal, bc - br + li_r, -jnp.inf)
        inter = bc + m_prev
        m_t = jnp.maximum(inter, jnp.max(dmat, axis=-1, keepdims=True))
        w_intra = jnp.exp(dmat - m_t)
        w_prev = jnp.exp(inter - m_t)
        qb = q.astype(BF16)
        s = lax.dot_general(qb, k.astype(BF16), _NT, preferred_element_type=F32) * w_intra
        tot = (jnp.dot(s.astype(BF16), v_aug, preferred_element_type=F32)
               + w_prev * jnp.dot(qb, c_scr[h].astype(BF16), preferred_element_type=F32))
        den = tot[:, D:D + 1]
        hh = tot[:, :D] / jnp.maximum(jnp.abs(den), jnp.exp(-m_t))

        b_last = bc[Lc - 1:Lc, :]
        g_c = b_last - bc + li_c
        m_new = jnp.maximum(b_last + m_prev, jnp.max(g_c, axis=0, keepdims=True))
        a = jnp.exp(b_last + m_prev - m_new)
        kw_t = (k * jnp.exp(g_c - m_new)).T.astype(BF16)
        c_scr[h] = a * c_scr[h] + jnp.dot(kw_t, v_aug, preferred_element_type=F32)
        m_scr[h] = jnp.broadcast_to(m_new, (8, LANES))

        hn = hh * lax.rsqrt(jnp.mean(hh * hh, axis=-1, keepdims=True) + NORM_EPS) * g_ref[0:1, h * D:(h + 1) * D]
        out_ref[0, :, h * D:(h + 1) * D] = (jax.nn.sigmoid(o_ref[0, :, h * D:(h + 1) * D]) * hn).astype(BF16)


def _mlstm(p3, conv_w, b_sm, m_norm, Lc=256):
    B, S, _ = p3.shape
    Lc = min(Lc, S)
    kern = functools.partial(_mlstm_kernel, Lc=Lc)
    return pl.pallas_call(
        kern,
        grid=(B, S // Lc),
        in_specs=[
            pl.BlockSpec((1, Lc, 2 * M_WIDTH), lambda b, c: (b, c, COL_MQK // (2 * M_WIDTH))),
            pl.BlockSpec((1, Lc, M_WIDTH), lambda b, c: (b, c, COL_MV // M_WIDTH)),
            pl.BlockSpec((1, Lc, M_WIDTH), lambda b, c: (b, c, COL_MO // M_WIDTH)),
            pl.BlockSpec((1, Lc, LANES), lambda b, c: (b, c, COL_SM // LANES)),
            _const_spec((CONV_W, 2 * M_WIDTH)),
            _const_spec((1, LANES)),
            _const_spec((1, M_WIDTH)),
        ],
        out_specs=pl.BlockSpec((1, Lc, M_WIDTH), lambda b, c: (b, c, 0)),
        out_shape=jax.ShapeDtypeStruct((B, S, M_WIDTH), BF16),
        scratch_shapes=[
            pltpu.VMEM((Lc + 8, 2 * M_WIDTH), F32),
            pltpu.VMEM((M_HEADS, M_HEAD_DIM, 2 * M_HEAD_DIM), F32),
            pltpu.VMEM((M_HEADS, 8, LANES), F32),
        ],
        compiler_params=_cparams(2),
        name="mlstm",
    )(p3, p3, p3, p3, conv_w, b_sm, m_norm)


def _moba_prep_kernel(q_ref, k_ref, v_ref, cos_ref, sin_ref, qg_ref, kg_ref, qa_ref, ka_ref, kmean_scr):
    i = pl.program_id(1)
    T = q_ref.shape[1]

    @pl.when(i == 0)
    def _():
        kmean_scr[...] = jnp.zeros_like(kmean_scr)

    cos, sin = cos_ref[...], sin_ref[...]
    qn = _rope(_rms_groups(q_ref[0], HEAD_DIM) * qg_ref[...], cos, sin)
    kn = _rope(_rms_groups(k_ref[0], HEAD_DIM) * kg_ref[...], cos, sin)
    v = v_ref[0]
    lane_w = _iota((T, ATT_WIDTH), 1)
    lane = _iota((T, LANES), 1)
    onehot = jnp.where(lane == i, 1.0, 0.0).astype(BF16)
    kmeans = kmean_scr[...]
    for h in range(ATT_HEADS):
        qm = jnp.where((lane_w >= h * HEAD_DIM) & (lane_w < (h + 1) * HEAD_DIM), qn, 0.0)
        gs = lax.dot_general(qm, kmeans, _NT, precision=_HI, preferred_element_type=F32)
        sel = _top_k_mask(jnp.where(lane < i, gs, NEG), MOBA_TOPK, rank_limit=i)
        bias = jnp.where((sel > 0) | (lane == i), 0.0, NEG)
        qa_ref[0, h, :, 0:LANES] = bias.astype(BF16)
        qa_ref[0, h, :, LANES:AUG_W] = (_head_piece(qn, h) * SCALE).astype(BF16)
        kp = kn[:, (h // 2) * LANES:(h // 2 + 1) * LANES]
        vp = v[:, (h // 2) * LANES:(h // 2 + 1) * LANES]
        if h % 2:
            kp = pltpu.roll(kp, HEAD_DIM, 1)
        else:
            vp = pltpu.roll(vp, HEAD_DIM, 1)
        ka_ref[0, h, :, 0:LANES] = onehot
        ka_ref[0, h, :, LANES:AUG_W] = jnp.where(lane < HEAD_DIM, kp, vp).astype(BF16)
    kmean_scr[pl.ds(i, 1), :] = jnp.mean(kn, axis=0, keepdims=True)


def _moba_prep(p3, cos, sin, qg, kg):
    B, S, _ = p3.shape
    T = MOBA_BLOCK
    blk = lambda col: pl.BlockSpec((1, T, ATT_WIDTH), lambda b, i: (b, i, col // ATT_WIDTH))
    tab = pl.BlockSpec((T, ATT_WIDTH), lambda b, i: (i, 0))
    aug = pl.BlockSpec((1, ATT_HEADS, T, AUG_W), lambda b, i: (b, 0, i, 0))
    return pl.pallas_call(
        _moba_prep_kernel,
        grid=(B, S // T),
        in_specs=[blk(COL_BQ), blk(COL_BK), blk(COL_BV), tab, tab,
                  _const_spec((1, ATT_WIDTH)), _const_spec((1, ATT_WIDTH))],
        out_specs=[aug, aug],
        out_shape=[jax.ShapeDtypeStruct((B, ATT_HEADS, S, AUG_W), BF16)] * 2,
        scratch_shapes=[pltpu.VMEM((LANES, ATT_WIDTH), F32)],
        compiler_params=_cparams(2),
        name="moba_prep",
    )(p3, p3, p3, cos, sin, qg, kg)


def _online_softmax_step(s, kv, m_scr, l_scr, acc_scr):
    m_prev = m_scr[...]
    m_new = jnp.maximum(m_prev, jnp.max(s, axis=-1, keepdims=True))
    alpha = jnp.exp(m_prev - m_new)
    p = jnp.exp(s - m_new)
    l_scr[...] = alpha * l_scr[...] + jnp.sum(p, axis=-1, keepdims=True)
    acc_scr[...] = alpha * acc_scr[...] + jnp.dot(p.astype(BF16), kv, preferred_element_type=F32)
    m_scr[...] = m_new


def _softmax_init(m_scr, l_scr, acc_scr):
    m_scr[...] = jnp.full(m_scr.shape, NEG, F32)
    l_scr[...] = jnp.zeros_like(l_scr)
    acc_scr[...] = jnp.zeros_like(acc_scr)


def _pack_head_pairs(o, T):
    lane = _iota((T, LANES), 1)
    return [jnp.where(lane < HEAD_DIM, pltpu.roll(o[2 * j * T:(2 * j + 1) * T], HEAD_DIM, 1),
                      o[(2 * j + 1) * T:(2 * j + 2) * T]) for j in range(o.shape[0] // (2 * T))]


def _flash_kernel(q_ref, k_ref, o_ref, m_scr, l_scr, acc_scr, *, P, T, shared):
    qi = pl.program_id(1)
    _softmax_init(m_scr, l_scr, acc_scr)
    n_chain = 1 if shared else P
    rows = P * T // n_chain
    qs = [q_ref[0].reshape(P * T, AUG_W)] if shared else [q_ref[0, c] for c in range(P)]
    row = _iota((rows, T), 0) % T
    col = _iota((rows, T), 1)

    def tile(j, diagonal):
        for c in range(n_chain):
            kt = k_ref[0, c, pl.ds(pl.multiple_of(j * T, T), T), :]
            s = lax.dot_general(qs[c], kt, _NT, preferred_element_type=F32)
            if diagonal:
                s = jnp.where(col <= row, s, NEG)
            sl = pl.ds(c * rows, rows)
            _online_softmax_step(s, kt[:, LANES:AUG_W], m_scr.at[sl], l_scr.at[sl], acc_scr.at[sl])

    def body(j, carry):
        tile(j, False)
        return carry

    lax.fori_loop(0, qi, body, 0)
    tile(qi, True)
    for j, pair in enumerate(_pack_head_pairs(acc_scr[...] / l_scr[...], T)):
        o_ref[0, :, j * LANES:(j + 1) * LANES] = pair.astype(o_ref.dtype)


def _flash(q_aug, k_aug, out_dtype, P, T=256):
    B, H, S, _ = q_aug.shape
    shared = k_aug.shape[1] == 1
    T = min(T, S)
    n = H // P
    kern = functools.partial(_flash_kernel, P=P, T=T, shared=shared)
    return pl.pallas_call(
        kern,
        grid=(B * n, S // T),
        in_specs=[pl.BlockSpec((1, P, T, AUG_W), lambda g, i: (g // n, g % n, i, 0)),
                  pl.BlockSpec((1, 1 if shared else P, S, AUG_W), lambda g, i: (g // n, 0 if shared else g % n, 0, 0))],
        out_specs=pl.BlockSpec((1, T, P * HEAD_DIM), lambda g, i: (g // n, i, g % n)),
        out_shape=jax.ShapeDtypeStruct((B, S, H * HEAD_DIM), out_dtype),
        scratch_shapes=[pltpu.VMEM((P * T, 1), F32), pltpu.VMEM((P * T, 1), F32),
                        pltpu.VMEM((P * T, LANES), F32)],
        compiler_params=_cparams(2),
        name="flash",
    )(q_aug, k_aug)


def _window_kernel(q_ref, k_ref, oc_ref, os_ref, sm_ref, y_ref, m_scr, l_scr, acc_scr, *, T):
    R = ATT_HEADS
    qi = pl.program_id(1)
    q = q_ref[0, :, :, LANES:AUG_W].reshape(R * T, LANES)
    _softmax_init(m_scr, l_scr, acc_scr)
    tq = qi * T + _iota((R * T, T), 0) % T
    col = _iota((R * T, T), 1)

    def body(j, carry):
        kt = k_ref[0, pl.ds(pl.multiple_of(j * T, T), T), :]
        s = lax.dot_general(q, kt, _NT, preferred_element_type=F32)
        key = j * T + col
        s = jnp.where((key <= tq) & (key > tq - WINDOW), s, NEG)
        _online_softmax_step(s, kt, m_scr, l_scr, acc_scr)
        return carry

    lax.fori_loop(jnp.maximum(qi - (WINDOW + T - 1) // T, 0), qi + 1, body, 0)
    sig = jax.nn.sigmoid(sm_ref[0])
    lane = _iota((T, LANES), 1)
    for j, o_w in enumerate(_pack_head_pairs(acc_scr[...] / l_scr[...], T)):
        def gate(branch):
            c = SM_G + branch * R + 2 * j
            return jnp.where(lane < HEAD_DIM, sig[:, c:c + 1], sig[:, c + 1:c + 2])
        sl = slice(j * LANES, (j + 1) * LANES)
        y_ref[0, :, sl] = (gate(0) * oc_ref[0, :, sl] + gate(1) * os_ref[0, :, sl] + gate(2) * o_w).astype(y_ref.dtype)


def _window_combine(q_aug, kw_aug, o_c, o_s, p3, T=256):
    B, R, S, _ = q_aug.shape
    T = min(T, S)
    packed = pl.BlockSpec((1, T, ATT_WIDTH), lambda b, i: (b, i, 0))
    return pl.pallas_call(
        functools.partial(_window_kernel, T=T),
        grid=(B, S // T),
        in_specs=[pl.BlockSpec((1, R, T, AUG_W), lambda b, i: (b, 0, i, 0)),
                  pl.BlockSpec((1, S, LANES), lambda b, i: (b, 0, 0)),
                  packed, packed,
                  pl.BlockSpec((1, T, LANES), lambda b, i: (b, i, COL_SM // LANES))],
        out_specs=packed,
        out_shape=jax.ShapeDtypeStruct((B, S, ATT_WIDTH), BF16),
        scratch_shapes=[pltpu.VMEM((R * T, 1), F32), pltpu.VMEM((R * T, 1), F32),
                        pltpu.VMEM((R * T, LANES), F32)],
        compiler_params=_cparams(2),
        name="nsa_window",
    )(q_aug, kw_aug, o_c, o_s, p3)


def _compress_kernel(t_ref, pea_ref, peb_ref, w1a_ref, w1b_ref, w2_ref, kg_ref, o_ref):
    t = t_ref[0]
    n = t.shape[0]
    a = jnp.dot((t + pea_ref[...]).astype(BF16), w1a_ref[...], preferred_element_type=F32)
    b = jnp.dot((t + peb_ref[...]).astype(BF16), w1b_ref[...], preferred_element_type=F32)
    hid = a + pltpu.roll(b, n - 1, 0)
    hid = hid * jax.nn.sigmoid(hid)
    kv = jnp.dot(hid.astype(BF16), w2_ref[...], preferred_element_type=F32)
    lane = _iota(kv.shape, 1)
    ms = jnp.sum(jnp.where(lane < HEAD_DIM, kv * kv, 0.0), axis=-1, keepdims=True) * (1.0 / HEAD_DIM)
    o_ref[0] = jnp.where(lane < HEAD_DIM, kv * lax.rsqrt(ms + NORM_EPS) * kg_ref[...], kv).astype(o_ref.dtype)


def _compress(t2, pea, peb, w1a, w1b, w2, kg):
    B, n, W = t2.shape
    return pl.pallas_call(
        _compress_kernel,
        grid=(B,),
        in_specs=[pl.BlockSpec((1, n, W), lambda b: (b, 0, 0)),
                  _const_spec((1, W)), _const_spec((1, W)),
                  _const_spec(w1a.shape), _const_spec(w1b.shape), _const_spec(w2.shape),
                  _const_spec((1, LANES))],
        out_specs=pl.BlockSpec((1, n, LANES), lambda b: (b, 0, 0)),
        out_shape=jax.ShapeDtypeStruct((B, n, LANES), BF16),
        compiler_params=_cparams(1),
        name="nsa_compress",
    )(t2, pea, peb, w1a, w1b, w2, kg)


def _nsa_prep_kernel(q_ref, kv_ref, cos_ref, sin_ref, qg_ref, kg_ref, kcv_ref, ov_ref,
                     oc_ref, qa_ref, ks_ref, kw_ref, *, n_sel):
    i = pl.program_id(1)
    T = q_ref.shape[1]
    nc = kcv_ref.shape[1]
    cos, sin = cos_ref[...], sin_ref[...]
    lane = _iota((T, LANES), 1)
    tq = i * T + _iota((T, 1), 0)

    qn = _rms_groups(q_ref[0], HEAD_DIM) * qg_ref[...]
    qr = _rope(qn, cos, sin)

    kcv = kcv_ref[0]
    valid = _iota((T, nc), 1) * CMP_STRIDE + (CMP_LEN - 1) <= tq
    psum = jnp.zeros((T, nc), F32)
    o_heads = []
    for h in range(ATT_HEADS):
        qh = (_head_piece(qn, h) * SCALE).astype(BF16)
        s = jnp.where(valid, lax.dot_general(qh, kcv, _NT, preferred_element_type=F32), NEG)
        e = jnp.exp(s - jnp.max(s, axis=-1, keepdims=True))
        p = jnp.where(valid, e / jnp.sum(e, axis=-1, keepdims=True), 0.0)
        o_heads.append(jnp.dot(p.astype(BF16), kcv, preferred_element_type=F32))
        psum = psum + p
    for j, pair in enumerate(_pack_head_pairs(jnp.concatenate(o_heads, axis=0), T)):
        oc_ref[0, :, j * LANES:(j + 1) * LANES] = pair

    imp = jnp.dot(psum, ov_ref[...], precision=_HI, preferred_element_type=F32)
    blk_q = tq // SEL_BLOCK
    causal_blk = lane <= blk_q
    forced = causal_blk & ((lane == 0) | (lane >= blk_q - 1))
    imp = jnp.where(forced, BIG, jnp.where(causal_blk, imp, NEG))
    imp = jnp.where(lane < n_sel, imp, -jnp.inf)
    sel = _top_k_mask(imp, min(SEL_TOPK, n_sel))
    bias = jnp.where(sel > 0, 0.0, NEG).astype(BF16)
    for h in range(ATT_HEADS):
        qa_ref[0, h, :, 0:LANES] = bias
        qa_ref[0, h, :, LANES:AUG_W] = (_head_piece(qr, h) * SCALE).astype(BF16)

    cos_k = jnp.where(lane < HEAD_DIM, cos[:, :LANES], 1.0)
    sin_k = jnp.where(lane < HEAD_DIM, sin[:, :LANES], 0.0)

    def key_pair(x, gain):
        ms = jnp.sum(jnp.where(lane < HEAD_DIM, x * x, 0.0), axis=-1, keepdims=True) * (1.0 / HEAD_DIM)
        kn = _rope(x * lax.rsqrt(ms + NORM_EPS) * gain, cos_k, sin_k)
        return jnp.where(lane < HEAD_DIM, kn, x).astype(BF16)

    ks_ref[0, :, 0:LANES] = jnp.where(lane == tq // SEL_BLOCK, 1.0, 0.0).astype(BF16)
    ks_ref[0, :, LANES:AUG_W] = key_pair(kv_ref[0, :, LANES:2 * LANES], kg_ref[0:1, :])
    kw_ref[0] = key_pair(kv_ref[0, :, 2 * LANES:3 * LANES], kg_ref[1:2, :])


def _nsa_prep(p3, cos, sin, qg, kg2, kcv, overlap, T=256):
    B, S, _ = p3.shape
    T = min(T, S)
    nc = kcv.shape[1]
    n_sel = S // SEL_BLOCK
    tab = pl.BlockSpec((T, ATT_WIDTH), lambda b, i: (i, 0))
    return pl.pallas_call(
        functools.partial(_nsa_prep_kernel, n_sel=n_sel),
        grid=(B, S // T),
        in_specs=[pl.BlockSpec((1, T, ATT_WIDTH), lambda b, i: (b, i, COL_NQ // ATT_WIDTH)),
                  pl.BlockSpec((1, T, 3 * LANES), lambda b, i: (b, i, COL_NKV // (3 * LANES))),
                  tab, tab,
                  _const_spec((1, ATT_WIDTH)), _const_spec((2, LANES)),
                  pl.BlockSpec((1, nc, LANES), lambda b, i: (b, 0, 0)),
                  _const_spec((nc, LANES))],
        out_specs=[pl.BlockSpec((1, T, ATT_WIDTH), lambda b, i: (b, i, 0)),
                   pl.BlockSpec((1, ATT_HEADS, T, AUG_W), lambda b, i: (b, 0, i, 0)),
                   pl.BlockSpec((1, T, AUG_W), lambda b, i: (b, i, 0)),
                   pl.BlockSpec((1, T, LANES), lambda b, i: (b, i, 0))],
        out_shape=[jax.ShapeDtypeStruct((B, S, ATT_WIDTH), F32),
                   jax.ShapeDtypeStruct((B, ATT_HEADS, S, AUG_W), BF16),
                   jax.ShapeDtypeStruct((B, S, AUG_W), BF16),
                   jax.ShapeDtypeStruct((B, S, LANES), BF16)],
        compiler_params=_cparams(2),
        name="nsa_prep",
    )(p3, p3, cos, sin, qg, kg2, kcv, overlap)


def _out_ffn_kernel(x_ref, ym_ref, yb_ref, yn_ref, wo_ref, g_ref, w1_ref, w2_ref, o_ref, *, fc):
    mix = jnp.concatenate([ym_ref[...], yb_ref[...], yn_ref[...]], axis=1)
    x = x_ref[...] + jnp.dot(mix, wo_ref[...], preferred_element_type=F32)
    hb = (x * lax.rsqrt(jnp.mean(x * x, axis=-1, keepdims=True) + NORM_EPS) * g_ref[...]).astype(BF16)
    acc = x
    for c in range(w1_ref.shape[1] // fc):
        u = jnp.maximum(jnp.dot(hb, w1_ref[:, c * fc:(c + 1) * fc], preferred_element_type=F32), 0.0)
        acc = acc + jnp.dot((u * u).astype(BF16), w2_ref[c * fc:(c + 1) * fc, :], preferred_element_type=F32)
    o_ref[...] = acc


def _out_ffn(x2d, y_m, y_b, y_n, wo, g, w1, w2, tm=256, fc=1024):
    M, D = x2d.shape
    tm = min(tm, M)
    rows = lambda w: pl.BlockSpec((tm, w), lambda i: (i, 0))
    return pl.pallas_call(
        functools.partial(_out_ffn_kernel, fc=fc),
        grid=(M // tm,),
        in_specs=[rows(D), rows(M_WIDTH), rows(ATT_WIDTH), rows(ATT_WIDTH),
                  _const_spec(wo.shape), _const_spec((1, D)), _const_spec(w1.shape), _const_spec(w2.shape)],
        out_specs=rows(D),
        out_shape=jax.ShapeDtypeStruct((M, D), F32),
        compiler_params=_cparams(1),
        name="out_ffn",
    )(x2d, y_m, y_b, y_n, wo, g, w1, w2)


def _rope_tables(S):
    inv_freq = jnp.exp(-math.log(ROPE_THETA) * jnp.arange(ROT_HALF, dtype=F32) * (2.0 / ROT_DIM))
    ang = jnp.arange(S, dtype=F32)[:, None] * inv_freq[None, :]
    cos, sin = jnp.cos(ang), jnp.sin(ang)
    rest = HEAD_DIM - ROT_DIM
    cos64 = jnp.concatenate([cos, cos, jnp.ones((S, rest), F32)], axis=1)
    sin64 = jnp.concatenate([-sin, sin, jnp.zeros((S, rest), F32)], axis=1)
    return jnp.tile(cos64, (1, ATT_HEADS)), jnp.tile(sin64, (1, ATT_HEADS))


def _overlap_matrix(nc):
    c_start = np.arange(nc)[:, None] * CMP_STRIDE
    s_start = np.arange(LANES)[None, :] * SEL_BLOCK
    return jnp.asarray(((c_start < s_start + SEL_BLOCK) & (c_start + CMP_LEN > s_start)).astype(np.float32))


def _permute_w_in(w):
    D = w.shape[0]
    m_end = 4 * M_WIDTH
    att0 = m_end + 2 * M_HEADS
    att1 = att0 + 4 * ATT_WIDTH + 6 * HEAD_DIM
    used = att1 - att0 + m_end + 2 * M_HEADS + 3 * ATT_HEADS
    return jnp.concatenate([w[:, :m_end], w[:, att0:att1], w[:, m_end:att0], w[:, att1:],
                            jnp.zeros((D, P_W - used), w.dtype)], axis=1)


def _compress_weights(pe, w1, w2):
    half = CMP_LEN // 2
    pe_r = jnp.concatenate([pe[0], pe[1]], axis=-1)
    pea = pe_r[:half].reshape(1, half * LANES)
    peb = pe_r[half:].reshape(1, half * LANES)
    w1r = w1.reshape(2, CMP_LEN, HEAD_DIM, CMP_HIDDEN)
    z = jnp.zeros_like(w1r[0])
    wk = jnp.concatenate([w1r[0], z], axis=-1)
    wv = jnp.concatenate([z, w1r[1]], axis=-1)
    wboth = jnp.concatenate([wk, wv], axis=1)
    w1a = wboth[:half].reshape(half * LANES, 2 * CMP_HIDDEN).astype(BF16)
    w1b = wboth[half:].reshape(half * LANES, 2 * CMP_HIDDEN).astype(BF16)
    z2 = jnp.zeros_like(w2[0])
    w2bd = jnp.concatenate([jnp.concatenate([w2[0], z2], axis=1),
                            jnp.concatenate([z2, w2[1]], axis=1)], axis=0).astype(BF16)
    return pea, peb, w1a, w1b, w2bd


def _pad_lanes(v, width=LANES):
    return jnp.concatenate([v, jnp.zeros((width - v.shape[0],), v.dtype)])[None, :]


def kernel(x, w_in, b_if, conv_qk, m_norm, moba_qk_norm, nsa_q_norm, nsa_k_norm, cmp_pe, cmp_w1, cmp_w2,
           w_out, norm_mix, norm_ffn, w_ff1, w_ff2):
    B, S, D = x.shape
    depth = w_in.shape[0]
    cos, sin = _rope_tables(S)
    overlap = _overlap_matrix(S // CMP_STRIDE)
    x2d = x.reshape(B * S, D)
    for l in range(depth):
        p3 = _proj(x2d, norm_mix[l][None, :], _permute_w_in(w_in[l]).astype(BF16)).reshape(B, S, P_W)

        y_m = _mlstm(p3, conv_qk[l], _pad_lanes(b_if[l]), m_norm[l][None, :])

        tile_g = lambda g: jnp.tile(g, ATT_HEADS)[None, :]
        qa_b, ka_b = _moba_prep(p3, cos, sin, tile_g(moba_qk_norm[l, 0]), tile_g(moba_qk_norm[l, 1]))
        y_b = _flash(qa_b, ka_b, BF16, P=2)

        t2 = p3[:, :, COL_NKV:COL_NKV + LANES].reshape(B, S // CMP_STRIDE, CMP_STRIDE * LANES)
        pea, peb, w1a, w1b, w2bd = _compress_weights(cmp_pe[l], cmp_w1[l], cmp_w2[l])
        kcv = _compress(t2, pea, peb, w1a, w1b, w2bd, _pad_lanes(nsa_k_norm[l, 0]))
        kg2 = jnp.concatenate([_pad_lanes(nsa_k_norm[l, 1]), _pad_lanes(nsa_k_norm[l, 2])], axis=0)
        o_c, qa_n, ks_aug, kw_aug = _nsa_prep(p3, cos, sin, tile_g(nsa_q_norm[l]), kg2, kcv, overlap)
        o_s = _flash(qa_n, ks_aug[:, None], F32, P=ATT_HEADS)
        y_n = _window_combine(qa_n, kw_aug, o_c, o_s, p3)

        x2d = _out_ffn(x2d, y_m.reshape(B * S, M_WIDTH), y_b.reshape(B * S, ATT_WIDTH), y_n.reshape(B * S, ATT_WIDTH),
                       w_out[l].astype(BF16), norm_ffn[l][None, :], w_ff1[l].astype(BF16), w_ff2[l].astype(BF16))
    return x2d.reshape(B, S, D)
```

```python
import functools
import math

import jax
import jax.numpy as jnp
import numpy as np
from jax import lax
from jax.experimental import pallas as pl
from jax.experimental.pallas import tpu as pltpu

F32 = jnp.float32
BF16 = jnp.bfloat16

HEAD_DIM = 64
M_HEADS = 4
M_HEAD_DIM = 128
M_WIDTH = M_HEADS * M_HEAD_DIM
CONV_W = 4
ATT_HEADS = 4
ATT_WIDTH = ATT_HEADS * HEAD_DIM
MOBA_BLOCK = 256
MOBA_TOPK = 3
CMP_LEN = 32
CMP_STRIDE = 16
CMP_HIDDEN = 128
SEL_BLOCK = 64
SEL_TOPK = 16
WINDOW = 512
ROPE_THETA = 500000.0
ROT_DIM = HEAD_DIM // 4
ROT_HALF = ROT_DIM // 2
NORM_EPS = 1e-6
NEG = -1e30
BIG = 1e9
SCALE = HEAD_DIM ** -0.5
SCALE_LOG2 = SCALE * math.log2(math.e)
BOUND_SLACK = 1.0 + 2.0 ** -7
L_TINY = 2.0 ** -100

LANES = 128
AUG_W = 2 * LANES
VT_ROWS = HEAD_DIM + 16
VMEM_LIMIT = 56 * 1024 * 1024

COL_MQK, COL_MV, COL_MO = 0, 1024, 1536
COL_BQ, COL_BK, COL_BV = 2048, 2304, 2560
COL_NQ, COL_NKV, COL_SM = 2816, 3072, 3456
P_W = 3584
SM_I, SM_F, SM_G = 0, 4, 8

_NT = (((1,), (1,)), ((), ()))
_HI = lax.Precision.HIGHEST


def _iota(shape, dim):
    return lax.broadcasted_iota(jnp.int32, shape, dim)


def _cparams(n_axes):
    return pltpu.CompilerParams(dimension_semantics=("arbitrary",) * n_axes,
                                vmem_limit_bytes=VMEM_LIMIT)


def _const_spec(shape):
    nd = len(shape)
    return pl.BlockSpec(shape, lambda *_: (0,) * nd)


def _rms_groups(x, width):
    T, W = x.shape
    x2 = x * x
    lane = _iota((T, W), 1)
    scale = None
    for h in range(W // width):
        r = lax.rsqrt(jnp.mean(x2[:, h * width:(h + 1) * width], axis=-1, keepdims=True) + NORM_EPS)
        scale = r if scale is None else jnp.where(lane >= h * width, r, scale)
    return x * scale


def _rope(x, cos, sin):
    W = x.shape[1]
    lane = _iota(x.shape, 1) % HEAD_DIM
    up = pltpu.roll(x, W - ROT_HALF, 1)
    dn = pltpu.roll(x, ROT_HALF, 1)
    return x * cos + jnp.where(lane < ROT_HALF, up, dn) * sin


def _head_piece(x, h):
    pair = x[:, (h // 2) * LANES:(h // 2 + 1) * LANES]
    if h % 2:
        pair = pltpu.roll(pair, HEAD_DIM, 1)
    return jnp.where(_iota(pair.shape, 1) < HEAD_DIM, pair, 0.0)


def _group_norms(x, width):
    x2 = x * x
    return [jnp.sqrt(jnp.sum(x2[:, h * width:(h + 1) * width], axis=-1, keepdims=True))
            for h in range(x.shape[1] // width)]


def _running_max(scr, idx, col):
    new = jnp.maximum(scr[idx][0:1, 0:1], jnp.max(col, axis=0, keepdims=True))
    scr[idx] = jnp.broadcast_to(new, scr.shape[1:])
    return new


def _value_tile_t(v_t):
    extra = jnp.where(_iota((VT_ROWS - HEAD_DIM, v_t.shape[1]), 0) == 0, 1.0, 0.0)
    return jnp.concatenate([v_t, extra], axis=0).astype(BF16)


def _top_k_mask(vals, k, rank_limit=None):
    lane = _iota(vals.shape, 1)
    sel = jnp.zeros(vals.shape, jnp.int32)
    for r in range(k):
        mx = jnp.max(vals, axis=-1, keepdims=True)
        idx = jnp.min(jnp.where(vals == mx, lane, LANES), axis=-1, keepdims=True)
        pick = lane == idx
        mark = 1 if rank_limit is None else jnp.where(rank_limit > r, 1, 0)
        sel = jnp.where(pick, mark, sel)
        vals = jnp.where(pick, -jnp.inf, vals)
    return sel


def _proj_kernel(x_ref, g_ref, w_ref, o_ref):
    x = x_ref[...]
    h = x * lax.rsqrt(jnp.mean(x * x, axis=-1, keepdims=True) + NORM_EPS) * g_ref[...]
    o_ref[...] = jnp.dot(h.astype(BF16), w_ref[...], preferred_element_type=F32)


def _proj(x2d, g, w, tm=256):
    M, D = x2d.shape
    N = w.shape[1]
    return pl.pallas_call(
        _proj_kernel,
        grid=(M // tm,),
        in_specs=[pl.BlockSpec((tm, D), lambda i: (i, 0)), _const_spec((1, D)), _const_spec((D, N))],
        out_specs=pl.BlockSpec((tm, N), lambda i: (i, 0)),
        out_shape=jax.ShapeDtypeStruct((M, N), F32),
        compiler_params=_cparams(1),
        name="in_proj",
    )(x2d, g, w)


def _mlstm_kernel(qk_ref, v_ref, o_ref, sm_ref, cw_ref, b_ref, g_ref, out_ref, xbuf, c_scr, m_scr, *, Lc):
    W2 = 2 * M_WIDTH
    D = M_HEAD_DIM

    @pl.when(pl.program_id(1) == 0)
    def _():
        xbuf[0:8, :] = jnp.zeros((8, W2), F32)
        c_scr[...] = jnp.zeros_like(c_scr)
        m_scr[...] = jnp.zeros_like(m_scr)

    x = qk_ref[0]
    xbuf[8:8 + Lc, :] = x
    y = cw_ref[0:1, :] * xbuf[8 - CONV_W + 1:8 - CONV_W + 1 + Lc, :]
    for j in range(1, CONV_W):
        y = y + cw_ref[j:j + 1, :] * xbuf[8 - CONV_W + 1 + j:8 - CONV_W + 1 + j + Lc, :]
    xbuf[0:8, :] = x[Lc - 8:Lc, :]
    qk = y * jax.nn.sigmoid(y)

    gi = sm_ref[0] + b_ref[...]
    lane = _iota((Lc, LANES), 1)
    lsig = jnp.minimum(gi, 0.0) - jnp.log1p(jnp.exp(-jnp.abs(gi)))
    gates = jnp.where(lane < SM_F, gi, jnp.where(lane < SM_F + M_HEADS, lsig, 0.0))
    gates_t = gates.T
    ri, ci = _iota((Lc, Lc), 0), _iota((Lc, Lc), 1)
    causal = ci <= ri
    b_col = jnp.dot(causal.astype(F32), gates, precision=_HI, preferred_element_type=F32)
    b_row = jnp.dot(gates_t[0:8, :], (ri <= ci).astype(F32), precision=_HI, preferred_element_type=F32)
    ones_col = jnp.where(lane == 0, 1.0, 0.0)

    for h in range(M_HEADS):
        q = qk[:, h * D:(h + 1) * D]
        k = qk[:, M_WIDTH + h * D:M_WIDTH + (h + 1) * D] * (D ** -0.5)
        v_aug = jnp.concatenate([v_ref[0, :, h * D:(h + 1) * D], ones_col], axis=1).astype(BF16)
        bc = b_col[:, SM_F + h:SM_F + h + 1]
        br = b_row[SM_F + h:SM_F + h + 1, :]
        li_r = gates_t[SM_I + h:SM_I + h + 1, :]
        li_c = gates[:, SM_I + h:SM_I + h + 1]
        m_prev = m_scr[h][0:1, 0:1]

        dmat = jnp.where(causal, bc - br + li_r, -jnp.inf)
        inter = bc + m_prev
        m_t = jnp.maximum(inter, jnp.max(dmat, axis=-1, keepdims=True))
        w_intra = jnp.exp(dmat - m_t)
        w_prev = jnp.exp(inter - m_t)
        qb = q.astype(BF16)
        s = lax.dot_general(qb, k.astype(BF16), _NT, preferred_element_type=F32) * w_intra
        tot = (jnp.dot(s.astype(BF16), v_aug, preferred_element_type=F32)
               + w_prev * jnp.dot(qb, c_scr[h].astype(BF16), preferred_element_type=F32))
        den = tot[:, D:D + 1]
        hh = tot[:, :D] / jnp.maximum(jnp.abs(den), jnp.exp(-m_t))

        b_last = bc[Lc - 1:Lc, :]
        g_c = b_last - bc + li_c
        m_new = jnp.maximum(b_last + m_prev, jnp.max(g_c, axis=0, keepdims=True))
        a = jnp.exp(b_last + m_prev - m_new)
        kw_t = (k * jnp.exp(g_c - m_new)).T.astype(BF16)
        c_scr[h] = a * c_scr[h] + jnp.dot(kw_t, v_aug, preferred_element_type=F32)
        m_scr[h] = jnp.broadcast_to(m_new, (8, LANES))

        hn = hh * lax.rsqrt(jnp.mean(hh * hh, axis=-1, keepdims=True) + NORM_EPS) * g_ref[0:1, h * D:(h + 1) * D]
        out_ref[0, :, h * D:(h + 1) * D] = (jax.nn.sigmoid(o_ref[0, :, h * D:(h + 1) * D]) * hn).astype(BF16)


def _mlstm(p3, conv_w, b_sm, m_norm, Lc=256):
    B, S, _ = p3.shape
    Lc = min(Lc, S)
    kern = functools.partial(_mlstm_kernel, Lc=Lc)
    return pl.pallas_call(
        kern,
        grid=(B, S // Lc),
        in_specs=[
            pl.BlockSpec((1, Lc, 2 * M_WIDTH), lambda b, c: (b, c, COL_MQK // (2 * M_WIDTH))),
            pl.BlockSpec((1, Lc, M_WIDTH), lambda b, c: (b, c, COL_MV // M_WIDTH)),
            pl.BlockSpec((1, Lc, M_WIDTH), lambda b, c: (b, c, COL_MO // M_WIDTH)),
            pl.BlockSpec((1, Lc, LANES), lambda b, c: (b, c, COL_SM // LANES)),
            _const_spec((CONV_W, 2 * M_WIDTH)),
            _const_spec((1, LANES)),
            _const_spec((1, M_WIDTH)),
        ],
        out_specs=pl.BlockSpec((1, Lc, M_WIDTH), lambda b, c: (b, c, 0)),
        out_shape=jax.ShapeDtypeStruct((B, S, M_WIDTH), BF16),
        scratch_shapes=[
            pltpu.VMEM((Lc + 8, 2 * M_WIDTH), F32),
            pltpu.VMEM((M_HEADS, M_HEAD_DIM, 2 * M_HEAD_DIM), F32),
            pltpu.VMEM((M_HEADS, 8, LANES), F32),
        ],
        compiler_params=_cparams(2),
        name="mlstm",
    )(p3, p3, p3, p3, conv_w, b_sm, m_norm)


def _moba_prep_kernel(q_ref, k_ref, v_ref, cos_ref, sin_ref, qg_ref, kg_ref, qt_ref, kp_ref, vt_ref,
                      kmean_scr, kmax_scr):
    i = pl.program_id(1)
    T = q_ref.shape[1]

    @pl.when(i == 0)
    def _():
        kmean_scr[...] = jnp.zeros_like(kmean_scr)
        kmax_scr[...] = jnp.zeros_like(kmax_scr)

    cos, sin = cos_ref[...], sin_ref[...]
    qn = _rope(_rms_groups(q_ref[0], HEAD_DIM) * qg_ref[...], cos, sin)
    kn = _rope(_rms_groups(k_ref[0], HEAD_DIM) * kg_ref[...], cos, sin)
    v_t = v_ref[0].T
    q2 = (qn * SCALE_LOG2).astype(BF16).astype(F32)
    kb = kn.astype(BF16).astype(F32)
    q_norms, k_norms = _group_norms(q2, HEAD_DIM), _group_norms(kb, HEAD_DIM)
    lane_w = _iota((T, ATT_WIDTH), 1)
    lane = _iota((T, LANES), 1)
    kmeans = kmean_scr[...]
    for h in range(ATT_HEADS):
        qm = jnp.where((lane_w >= h * HEAD_DIM) & (lane_w < (h + 1) * HEAD_DIM), qn, 0.0)
        gs = lax.dot_general(qm, kmeans, _NT, precision=_HI, preferred_element_type=F32)
        sel = _top_k_mask(jnp.where(lane < i, gs, NEG), MOBA_TOPK, rank_limit=i)
        bound = q_norms[h] * (_running_max(kmax_scr, h, k_norms[h]) * BOUND_SLACK)
        qt_ref[0, h, 0:LANES, :] = (jnp.where((sel > 0) | (lane == i), 0.0, NEG) - bound).T.astype(BF16)
        own_half = (lane >= HEAD_DIM) if h % 2 else (lane < HEAD_DIM)
        qt_ref[0, h, LANES:AUG_W, :] = jnp.where(own_half, q2[:, (h // 2) * LANES:(h // 2 + 1) * LANES], 0.0).T.astype(BF16)
        vt_ref[0, h, 0] = _value_tile_t(v_t[h * HEAD_DIM:(h + 1) * HEAD_DIM, :])
    for j in range(ATT_HEADS // 2):
        kp_ref[0, j] = kb[:, j * LANES:(j + 1) * LANES].astype(BF16)
    kmean_scr[pl.ds(i, 1), :] = jnp.mean(kn, axis=0, keepdims=True)


def _moba_prep(p3, cos, sin, qg, kg):
    B, S, _ = p3.shape
    T = MOBA_BLOCK
    blk = lambda col: pl.BlockSpec((1, T, ATT_WIDTH), lambda b, i: (b, i, col // ATT_WIDTH))
    tab = pl.BlockSpec((T, ATT_WIDTH), lambda b, i: (i, 0))
    return pl.pallas_call(
        _moba_prep_kernel,
        grid=(B, S // T),
        in_specs=[blk(COL_BQ), blk(COL_BK), blk(COL_BV), tab, tab,
                  _const_spec((1, ATT_WIDTH)), _const_spec((1, ATT_WIDTH))],
        out_specs=[pl.BlockSpec((1, ATT_HEADS, AUG_W, T), lambda b, i: (b, 0, 0, i)),
                   pl.BlockSpec((1, ATT_HEADS // 2, T, LANES), lambda b, i: (b, 0, i, 0)),
                   pl.BlockSpec((1, ATT_HEADS, 1, VT_ROWS, T), lambda b, i: (b, 0, i, 0, 0))],
        out_shape=[jax.ShapeDtypeStruct((B, ATT_HEADS, AUG_W, S), BF16),
                   jax.ShapeDtypeStruct((B, ATT_HEADS // 2, S, LANES), BF16),
                   jax.ShapeDtypeStruct((B, ATT_HEADS, S // T, VT_ROWS, T), BF16)],
        scratch_shapes=[pltpu.VMEM((LANES, ATT_WIDTH), F32), pltpu.VMEM((ATT_HEADS, 8, LANES), F32)],
        compiler_params=_cparams(2),
        name="moba_prep",
    )(p3, p3, p3, cos, sin, qg, kg)


def _attend(tile_scores, tile_pv, lo, hi, acc_scr, m_scr):
    n_chain = acc_scr.shape[0]

    def run(exact):
        acc_scr[...] = jnp.zeros_like(acc_scr)
        if exact:
            m_scr[...] = jnp.full(m_scr.shape, NEG, F32)

        def tile(j, last):
            for c in range(n_chain):
                s_t = tile_scores(c, j, last)
                if exact:
                    m_prev = m_scr[c]
                    m_new = jnp.maximum(m_prev, jnp.max(s_t, axis=0, keepdims=True))
                    acc_scr[c] = (acc_scr[c] * jnp.exp2(m_prev - m_new)
                                  + tile_pv(c, j, jnp.exp2(s_t - m_new).astype(BF16)))
                    m_scr[c] = m_new
                else:
                    acc_scr[c] += tile_pv(c, j, jnp.exp2(s_t).astype(BF16))

        def pair(i, carry):
            tile(lo + 2 * i, False)
            tile(lo + 2 * i + 1, False)
            return carry

        lax.fori_loop(0, (hi - lo) // 2, pair, 0)

        @pl.when((hi - lo) % 2 == 1)
        def _():
            tile(hi - 1, False)

        tile(hi, True)

    run(False)
    l_min = jnp.min(acc_scr[:, HEAD_DIM:HEAD_DIM + 1, :])

    @pl.when(jnp.logical_not(l_min >= L_TINY))
    def _():
        run(True)


def _head_pair_rows(o_a, o_b):
    return jnp.concatenate([o_a, o_b], axis=0).T


def _normalised_heads(acc_scr, T):
    heads = []
    for c in range(acc_scr.shape[0]):
        o = acc_scr[c, 0:HEAD_DIM, :] / acc_scr[c, HEAD_DIM:HEAD_DIM + 1, :]
        heads += [o[:, h * T:(h + 1) * T] for h in range(o.shape[1] // T)]
    return heads


def _flash_kernel(qt_ref, k_ref, vt_ref, o_ref, acc_scr, m_scr, *, T, mask_block):
    qi = pl.program_id(1)
    P, G, VH = qt_ref.shape[1], k_ref.shape[1], vt_ref.shape[1]
    hc = P // G
    N = hc * T
    q_cat = [jnp.concatenate([qt_ref[0, c * hc + h] for h in range(hc)], axis=1) for c in range(G)]
    key, qry = _iota((T, N), 0), _iota((T, N), 1) % T
    lane, row = _iota((T, LANES), 1), _iota((T, LANES), 0)

    def tile_scores(c, j, last):
        onehot = jnp.where(lane == (j * T + row) // mask_block, 1.0, 0.0).astype(BF16)
        kt = jnp.concatenate([onehot, k_ref[0, c, pl.ds(pl.multiple_of(j * T, T), T), :]], axis=1)
        s_t = jnp.dot(kt, q_cat[c], preferred_element_type=F32)
        return jnp.where(key <= qry, s_t, NEG) if last else s_t

    def tile_pv(c, j, p):
        if VH == G:
            return jnp.dot(vt_ref[0, c, j], p, preferred_element_type=F32)
        return jnp.concatenate([jnp.dot(vt_ref[0, c * hc + h, j], p[:, h * T:(h + 1) * T],
                                        preferred_element_type=F32) for h in range(hc)], axis=1)

    _attend(tile_scores, tile_pv, 0, qi, acc_scr, m_scr)
    heads = _normalised_heads(acc_scr, T)
    for j in range(P // 2):
        o_ref[0, :, j * LANES:(j + 1) * LANES] = _head_pair_rows(heads[2 * j], heads[2 * j + 1]).astype(o_ref.dtype)


def _flash(q_t, k, v_t, out_dtype, mask_block, T=256):
    B, P, _, S = q_t.shape
    T = min(T, S)
    KH, VH = k.shape[1], v_t.shape[1]
    assert v_t.shape == (B, VH, S // T, VT_ROWS, T) and k.shape == (B, KH, S, LANES)
    return pl.pallas_call(
        functools.partial(_flash_kernel, T=T, mask_block=mask_block),
        grid=(B, S // T),
        in_specs=[pl.BlockSpec((1, P, AUG_W, T), lambda b, i: (b, 0, 0, i)),
                  pl.BlockSpec((1, KH, S, LANES), lambda b, i: (b, 0, 0, 0)),
                  pl.BlockSpec((1, VH, S // T, VT_ROWS, T), lambda b, i: (b, 0, 0, 0, 0))],
        out_specs=pl.BlockSpec((1, T, P * HEAD_DIM), lambda b, i: (b, i, 0)),
        out_shape=jax.ShapeDtypeStruct((B, S, P * HEAD_DIM), out_dtype),
        scratch_shapes=[pltpu.VMEM((KH, VT_ROWS, P // KH * T), F32), pltpu.VMEM((KH, 1, P // KH * T), F32)],
        compiler_params=_cparams(2),
        name="flash",
    )(q_t, k, v_t)


def _window_kernel(qt_ref, k_ref, vt_ref, oc_ref, os_ref, sm_ref, y_ref, acc_scr, m_scr, *, T):
    R = ATT_HEADS
    qi = pl.program_id(1)
    q_cat = jnp.concatenate([qt_ref[0, h, LANES:AUG_W, :] for h in range(R)], axis=1)
    tq = qi * T + _iota((T, R * T), 1) % T
    row = _iota((T, R * T), 0)

    def tile_scores(c, j, last):
        kt = k_ref[0, pl.ds(pl.multiple_of(j * T, T), T), :]
        s_t = jnp.dot(kt, q_cat, preferred_element_type=F32)
        key = j * T + row
        return jnp.where((key <= tq) & (key > tq - WINDOW), s_t, NEG)

    def tile_pv(c, j, p):
        return jnp.dot(vt_ref[0, j], p, preferred_element_type=F32)

    _attend(tile_scores, tile_pv, jnp.maximum(qi - (WINDOW + T - 1) // T, 0), qi, acc_scr, m_scr)
    sig = jax.nn.sigmoid(sm_ref[0])
    lane = _iota((T, LANES), 1)
    heads = _normalised_heads(acc_scr, T)
    for j in range(R // 2):
        o_w = _head_pair_rows(heads[2 * j], heads[2 * j + 1])

        def gate(branch):
            c = SM_G + branch * R + 2 * j
            return jnp.where(lane < HEAD_DIM, sig[:, c:c + 1], sig[:, c + 1:c + 2])
        sl = slice(j * LANES, (j + 1) * LANES)
        y_ref[0, :, sl] = (gate(0) * oc_ref[0, :, sl] + gate(1) * os_ref[0, :, sl] + gate(2) * o_w).astype(y_ref.dtype)


def _window_combine(q_t, kw, vw_t, o_c, o_s, p3, T=256):
    B, R, _, S = q_t.shape
    T = min(T, S)
    packed = pl.BlockSpec((1, T, ATT_WIDTH), lambda b, i: (b, i, 0))
    return pl.pallas_call(
        functools.partial(_window_kernel, T=T),
        grid=(B, S // T),
        in_specs=[pl.BlockSpec((1, R, AUG_W, T), lambda b, i: (b, 0, 0, i)),
                  pl.BlockSpec((1, S, LANES), lambda b, i: (b, 0, 0)),
                  pl.BlockSpec((1, S // T, VT_ROWS, T), lambda b, i: (b, 0, 0, 0)),
                  packed, packed,
                  pl.BlockSpec((1, T, LANES), lambda b, i: (b, i, COL_SM // LANES))],
        out_specs=packed,
        out_shape=jax.ShapeDtypeStruct((B, S, ATT_WIDTH), BF16),
        scratch_shapes=[pltpu.VMEM((1, VT_ROWS, R * T), F32), pltpu.VMEM((1, 1, R * T), F32)],
        compiler_params=_cparams(2),
        name="nsa_window",
    )(q_t, kw, vw_t, o_c, o_s, p3)


def _compress_kernel(t_ref, pea_ref, peb_ref, w1a_ref, w1b_ref, w2_ref, kg_ref, o_ref):
    t = t_ref[0]
    n = t.shape[0]
    a = jnp.dot((t + pea_ref[...]).astype(BF16), w1a_ref[...], preferred_element_type=F32)
    b = jnp.dot((t + peb_ref[...]).astype(BF16), w1b_ref[...], preferred_element_type=F32)
    hid = a + pltpu.roll(b, n - 1, 0)
    hid = hid * jax.nn.sigmoid(hid)
    kv = jnp.dot(hid.astype(BF16), w2_ref[...], preferred_element_type=F32)
    lane = _iota(kv.shape, 1)
    ms = jnp.sum(jnp.where(lane < HEAD_DIM, kv * kv, 0.0), axis=-1, keepdims=True) * (1.0 / HEAD_DIM)
    o_ref[0] = jnp.where(lane < HEAD_DIM, kv * lax.rsqrt(ms + NORM_EPS) * kg_ref[...], kv).astype(o_ref.dtype)


def _compress(t2, pea, peb, w1a, w1b, w2, kg):
    B, n, W = t2.shape
    return pl.pallas_call(
        _compress_kernel,
        grid=(B,),
        in_specs=[pl.BlockSpec((1, n, W), lambda b: (b, 0, 0)),
                  _const_spec((1, W)), _const_spec((1, W)),
                  _const_spec(w1a.shape), _const_spec(w1b.shape), _const_spec(w2.shape),
                  _const_spec((1, LANES))],
        out_specs=pl.BlockSpec((1, n, LANES), lambda b: (b, 0, 0)),
        out_shape=jax.ShapeDtypeStruct((B, n, LANES), BF16),
        compiler_params=_cparams(1),
        name="nsa_compress",
    )(t2, pea, peb, w1a, w1b, w2, kg)


def _nsa_prep_kernel(q_ref, kv_ref, cos_ref, sin_ref, qg_ref, kg_ref, kcv_ref, ov_ref,
                     oc_ref, qt_ref, ks_ref, kw_ref, vst_ref, vwt_ref, kmax_scr, *, n_sel):
    i = pl.program_id(1)
    T = q_ref.shape[1]
    nc = kcv_ref.shape[1]

    @pl.when(i == 0)
    def _():
        kmax_scr[...] = jnp.zeros_like(kmax_scr)

    cos, sin = cos_ref[...], sin_ref[...]
    lane = _iota((T, LANES), 1)
    tq = i * T + _iota((T, 1), 0)

    qn = _rms_groups(q_ref[0], HEAD_DIM) * qg_ref[...]
    qr = _rope(qn, cos, sin)

    kcv = kcv_ref[0]
    valid = _iota((T, nc), 1) * CMP_STRIDE + (CMP_LEN - 1) <= tq
    psum = jnp.zeros((T, nc), F32)
    o_heads = []
    for h in range(ATT_HEADS):
        qh = (_head_piece(qn, h) * SCALE).astype(BF16)
        s = jnp.where(valid, lax.dot_general(qh, kcv, _NT, preferred_element_type=F32), NEG)
        e = jnp.exp(s - jnp.max(s, axis=-1, keepdims=True))
        p = jnp.where(valid, e / jnp.sum(e, axis=-1, keepdims=True), 0.0)
        o_heads.append(jnp.dot(p.astype(BF16), kcv, preferred_element_type=F32))
        psum = psum + p
    for j in range(ATT_HEADS // 2):
        oc_ref[0, :, j * LANES:(j + 1) * LANES] = jnp.where(
            lane < HEAD_DIM, pltpu.roll(o_heads[2 * j], HEAD_DIM, 1), o_heads[2 * j + 1])

    imp = jnp.dot(psum, ov_ref[...], precision=_HI, preferred_element_type=F32)
    blk_q = tq // SEL_BLOCK
    causal_blk = lane <= blk_q
    forced = causal_blk & ((lane == 0) | (lane >= blk_q - 1))
    imp = jnp.where(forced, BIG, jnp.where(causal_blk, imp, NEG))
    imp = jnp.where(lane < n_sel, imp, -jnp.inf)
    sel = _top_k_mask(imp, min(SEL_TOPK, n_sel))
    bias = jnp.where(sel > 0, 0.0, NEG)

    cos_k = jnp.where(lane < HEAD_DIM, cos[:, :LANES], 1.0)
    sin_k = jnp.where(lane < HEAD_DIM, sin[:, :LANES], 0.0)

    def key_pair(x, gain, idx):
        ms = jnp.sum(jnp.where(lane < HEAD_DIM, x * x, 0.0), axis=-1, keepdims=True) * (1.0 / HEAD_DIM)
        kn = _rope(x * lax.rsqrt(ms + NORM_EPS) * gain, cos_k, sin_k)
        kb = jnp.where(lane < HEAD_DIM, kn, 0.0).astype(BF16).astype(F32)
        return kb, _value_tile_t(x.T[HEAD_DIM:, :]), _running_max(kmax_scr, idx, _group_norms(kb, LANES)[0])

    ks, vst_ref[0, 0], ks_max = key_pair(kv_ref[0, :, LANES:2 * LANES], kg_ref[0:1, :], 0)
    kw, vwt_ref[0, 0], kw_max = key_pair(kv_ref[0, :, 2 * LANES:3 * LANES], kg_ref[1:2, :], 1)
    ks_ref[0] = ks.astype(BF16)
    kw_ref[0] = jnp.where(lane == HEAD_DIM, 1.0, kw).astype(BF16)

    q2 = (qr * SCALE_LOG2).astype(BF16).astype(F32)
    for h, q_norm in enumerate(_group_norms(q2, HEAD_DIM)):
        qt_ref[0, h, 0:LANES, :] = (bias - q_norm * (ks_max * BOUND_SLACK)).T.astype(BF16)
        qt_ref[0, h, LANES:AUG_W, :] = jnp.where(lane == HEAD_DIM, -q_norm * (kw_max * BOUND_SLACK),
                                                 _head_piece(q2, h)).T.astype(BF16)


def _nsa_prep(p3, cos, sin, qg, kg2, kcv, overlap, T=256):
    B, S, _ = p3.shape
    T = min(T, S)
    nc = kcv.shape[1]
    n_sel = S // SEL_BLOCK
    tab = pl.BlockSpec((T, ATT_WIDTH), lambda b, i: (i, 0))
    v_t = pl.BlockSpec((1, 1, VT_ROWS, T), lambda b, i: (b, i, 0, 0))
    v_t_shape = jax.ShapeDtypeStruct((B, S // T, VT_ROWS, T), BF16)
    return pl.pallas_call(
        functools.partial(_nsa_prep_kernel, n_sel=n_sel),
        grid=(B, S // T),
        in_specs=[pl.BlockSpec((1, T, ATT_WIDTH), lambda b, i: (b, i, COL_NQ // ATT_WIDTH)),
                  pl.BlockSpec((1, T, 3 * LANES), lambda b, i: (b, i, COL_NKV // (3 * LANES))),
                  tab, tab,
                  _const_spec((1, ATT_WIDTH)), _const_spec((2, LANES)),
                  pl.BlockSpec((1, nc, LANES), lambda b, i: (b, 0, 0)),
                  _const_spec((nc, LANES))],
        out_specs=[pl.BlockSpec((1, T, ATT_WIDTH), lambda b, i: (b, i, 0)),
                   pl.BlockSpec((1, ATT_HEADS, AUG_W, T), lambda b, i: (b, 0, 0, i)),
                   pl.BlockSpec((1, T, LANES), lambda b, i: (b, i, 0)),
                   pl.BlockSpec((1, T, LANES), lambda b, i: (b, i, 0)),
                   v_t, v_t],
        out_shape=[jax.ShapeDtypeStruct((B, S, ATT_WIDTH), F32),
                   jax.ShapeDtypeStruct((B, ATT_HEADS, AUG_W, S), BF16),
                   jax.ShapeDtypeStruct((B, S, LANES), BF16),
                   jax.ShapeDtypeStruct((B, S, LANES), BF16),
                   v_t_shape, v_t_shape],
        scratch_shapes=[pltpu.VMEM((2, 8, LANES), F32)],
        compiler_params=_cparams(2),
        name="nsa_prep",
    )(p3, p3, cos, sin, qg, kg2, kcv, overlap)


def _out_ffn_kernel(x_ref, ym_ref, yb_ref, yn_ref, wo_ref, g_ref, w1_ref, w2_ref, o_ref, *, fc):
    mix = jnp.concatenate([ym_ref[...], yb_ref[...], yn_ref[...]], axis=1)
    x = x_ref[...] + jnp.dot(mix, wo_ref[...], preferred_element_type=F32)
    hb = (x * lax.rsqrt(jnp.mean(x * x, axis=-1, keepdims=True) + NORM_EPS) * g_ref[...]).astype(BF16)
    acc = x
    for c in range(w1_ref.shape[1] // fc):
        u = jnp.maximum(jnp.dot(hb, w1_ref[:, c * fc:(c + 1) * fc], preferred_element_type=F32), 0.0)
        acc = acc + jnp.dot((u * u).astype(BF16), w2_ref[c * fc:(c + 1) * fc, :], preferred_element_type=F32)
    o_ref[...] = acc


def _out_ffn(x2d, y_m, y_b, y_n, wo, g, w1, w2, tm=256, fc=1024):
    M, D = x2d.shape
    tm = min(tm, M)
    rows = lambda w: pl.BlockSpec((tm, w), lambda i: (i, 0))
    return pl.pallas_call(
        functools.partial(_out_ffn_kernel, fc=fc),
        grid=(M // tm,),
        in_specs=[rows(D), rows(M_WIDTH), rows(ATT_WIDTH), rows(ATT_WIDTH),
                  _const_spec(wo.shape), _const_spec((1, D)), _const_spec(w1.shape), _const_spec(w2.shape)],
        out_specs=rows(D),
        out_shape=jax.ShapeDtypeStruct((M, D), F32),
        compiler_params=_cparams(1),
        name="out_ffn",
    )(x2d, y_m, y_b, y_n, wo, g, w1, w2)


def _rope_tables(S):
    inv_freq = jnp.exp(-math.log(ROPE_THETA) * jnp.arange(ROT_HALF, dtype=F32) * (2.0 / ROT_DIM))
    ang = jnp.arange(S, dtype=F32)[:, None] * inv_freq[None, :]
    cos, sin = jnp.cos(ang), jnp.sin(ang)
    rest = HEAD_DIM - ROT_DIM
    cos64 = jnp.concatenate([cos, cos, jnp.ones((S, rest), F32)], axis=1)
    sin64 = jnp.concatenate([-sin, sin, jnp.zeros((S, rest), F32)], axis=1)
    return jnp.tile(cos64, (1, ATT_HEADS)), jnp.tile(sin64, (1, ATT_HEADS))


def _overlap_matrix(nc):
    c_start = np.arange(nc)[:, None] * CMP_STRIDE
    s_start = np.arange(LANES)[None, :] * SEL_BLOCK
    return jnp.asarray(((c_start < s_start + SEL_BLOCK) & (c_start + CMP_LEN > s_start)).astype(np.float32))


def _permute_w_in(w):
    D = w.shape[0]
    m_end = 4 * M_WIDTH
    att0 = m_end + 2 * M_HEADS
    att1 = att0 + 4 * ATT_WIDTH + 6 * HEAD_DIM
    used = att1 - att0 + m_end + 2 * M_HEADS + 3 * ATT_HEADS
    return jnp.concatenate([w[:, :m_end], w[:, att0:att1], w[:, m_end:att0], w[:, att1:],
                            jnp.zeros((D, P_W - used), w.dtype)], axis=1)


def _compress_weights(pe, w1, w2):
    half = CMP_LEN // 2
    pe_r = jnp.concatenate([pe[0], pe[1]], axis=-1)
    pea = pe_r[:half].reshape(1, half * LANES)
    peb = pe_r[half:].reshape(1, half * LANES)
    w1r = w1.reshape(2, CMP_LEN, HEAD_DIM, CMP_HIDDEN)
    z = jnp.zeros_like(w1r[0])
    wk = jnp.concatenate([w1r[0], z], axis=-1)
    wv = jnp.concatenate([z, w1r[1]], axis=-1)
    wboth = jnp.concatenate([wk, wv], axis=1)
    w1a = wboth[:half].reshape(half * LANES, 2 * CMP_HIDDEN).astype(BF16)
    w1b = wboth[half:].reshape(half * LANES, 2 * CMP_HIDDEN).astype(BF16)
    z2 = jnp.zeros_like(w2[0])
    w2bd = jnp.concatenate([jnp.concatenate([w2[0], z2], axis=1),
                            jnp.concatenate([z2, w2[1]], axis=1)], axis=0).astype(BF16)
    return pea, peb, w1a, w1b, w2bd


def _pad_lanes(v, width=LANES):
    return jnp.concatenate([v, jnp.zeros((width - v.shape[0],), v.dtype)])[None, :]


def kernel(x, w_in, b_if, conv_qk, m_norm, moba_qk_norm, nsa_q_norm, nsa_k_norm, cmp_pe, cmp_w1, cmp_w2,
           w_out, norm_mix, norm_ffn, w_ff1, w_ff2):
    B, S, D = x.shape
    depth = w_in.shape[0]
    cos, sin = _rope_tables(S)
    overlap = _overlap_matrix(S // CMP_STRIDE)
    x2d = x.reshape(B * S, D)
    for l in range(depth):
        p3 = _proj(x2d, norm_mix[l][None, :], _permute_w_in(w_in[l]).astype(BF16)).reshape(B, S, P_W)

        y_m = _mlstm(p3, conv_qk[l], _pad_lanes(b_if[l]), m_norm[l][None, :])

        tile_g = lambda g: jnp.tile(g, ATT_HEADS)[None, :]
        qt_b, kp_b, vt_b = _moba_prep(p3, cos, sin, tile_g(moba_qk_norm[l, 0]), tile_g(moba_qk_norm[l, 1]))
        y_b = _flash(qt_b, kp_b, vt_b, BF16, MOBA_BLOCK)

        t2 = p3[:, :, COL_NKV:COL_NKV + LANES].reshape(B, S // CMP_STRIDE, CMP_STRIDE * LANES)
        pea, peb, w1a, w1b, w2bd = _compress_weights(cmp_pe[l], cmp_w1[l], cmp_w2[l])
        kcv = _compress(t2, pea, peb, w1a, w1b, w2bd, _pad_lanes(nsa_k_norm[l, 0]))
        kg2 = jnp.concatenate([_pad_lanes(nsa_k_norm[l, 1]), _pad_lanes(nsa_k_norm[l, 2])], axis=0)
        o_c, qt_n, ks, kw, vs_t, vw_t = _nsa_prep(p3, cos, sin, tile_g(nsa_q_norm[l]), kg2, kcv, overlap)
        o_s = _flash(qt_n, ks[:, None], vs_t[:, None], F32, SEL_BLOCK)
        y_n = _window_combine(qt_n, kw, vw_t, o_c, o_s, p3)

        x2d = _out_ffn(x2d, y_m.reshape(B * S, M_WIDTH), y_b.reshape(B * S, ATT_WIDTH), y_n.reshape(B * S, ATT_WIDTH),
                       w_out[l].astype(BF16), norm_ffn[l][None, :], w_ff1[l].astype(BF16), w_ff2[l].astype(BF16))
    return x2d.reshape(B, S, D)
```

```python
import functools
import math

import jax
import jax.numpy as jnp
import numpy as np
from jax import lax
from jax.experimental import pallas as pl
from jax.experimental.pallas import tpu as pltpu

F32 = jnp.float32
BF16 = jnp.bfloat16

HEAD_DIM = 64
M_HEADS = 4
M_HEAD_DIM = 128
M_WIDTH = M_HEADS * M_HEAD_DIM
CONV_W = 4
ATT_HEADS = 4
ATT_WIDTH = ATT_HEADS * HEAD_DIM
MOBA_BLOCK = 256
MOBA_TOPK = 3
CMP_LEN = 32
CMP_STRIDE = 16
CMP_HIDDEN = 128
SEL_BLOCK = 64
SEL_TOPK = 16
WINDOW = 512
ROPE_THETA = 500000.0
ROT_DIM = HEAD_DIM // 4
ROT_HALF = ROT_DIM // 2
NORM_EPS = 1e-6
NEG = -1e30
BIG = 1e9
SCALE = HEAD_DIM ** -0.5
SCALE_LOG2 = SCALE * math.log2(math.e)
BOUND_SLACK = 1.0 + 2.0 ** -7
L_TINY = 2.0 ** -100

LANES = 128
AUG_W = 2 * LANES
VT_ROWS = HEAD_DIM + 16
ATTEND_UNROLL = 4
VMEM_LIMIT = 56 * 1024 * 1024

COL_MQK, COL_MV, COL_MO = 0, 1024, 1536
COL_BQ, COL_BK, COL_BV = 2048, 2304, 2560
COL_NQ, COL_NKV, COL_SM = 2816, 3072, 3456
P_W = 3584
SM_I, SM_F, SM_G = 0, 4, 8
_W_ATT0 = 4 * M_WIDTH + 2 * M_HEADS
_W_ATT1 = _W_ATT0 + 4 * ATT_WIDTH + 6 * HEAD_DIM
W_IN_SECTIONS = ((0, 4 * M_WIDTH), (_W_ATT0, _W_ATT1 - _W_ATT0), (4 * M_WIDTH, 2 * M_HEADS), (_W_ATT1, 3 * ATT_HEADS))

_NT = (((1,), (1,)), ((), ()))
_HI = lax.Precision.HIGHEST


def _iota(shape, dim):
    return lax.broadcasted_iota(jnp.int32, shape, dim)


def _cparams(n_axes):
    return pltpu.CompilerParams(dimension_semantics=("arbitrary",) * n_axes,
                                vmem_limit_bytes=VMEM_LIMIT)


def _const_spec(shape):
    nd = len(shape)
    return pl.BlockSpec(shape, lambda *_: (0,) * nd)


def _rms_groups(x, width):
    T, W = x.shape
    x2 = x * x
    lane = _iota((T, W), 1)
    scale = None
    for h in range(W // width):
        r = lax.rsqrt(jnp.mean(x2[:, h * width:(h + 1) * width], axis=-1, keepdims=True) + NORM_EPS)
        scale = r if scale is None else jnp.where(lane >= h * width, r, scale)
    return x * scale


def _rope(x, cos, sin):
    W = x.shape[1]
    lane = _iota(x.shape, 1) % HEAD_DIM
    up = pltpu.roll(x, W - ROT_HALF, 1)
    dn = pltpu.roll(x, ROT_HALF, 1)
    return x * cos + jnp.where(lane < ROT_HALF, up, dn) * sin


def _head_piece(x, h):
    pair = x[:, (h // 2) * LANES:(h // 2 + 1) * LANES]
    if h % 2:
        pair = pltpu.roll(pair, HEAD_DIM, 1)
    return jnp.where(_iota(pair.shape, 1) < HEAD_DIM, pair, 0.0)


def _group_norms(x, width):
    x2 = x * x
    return [jnp.sqrt(jnp.sum(x2[:, h * width:(h + 1) * width], axis=-1, keepdims=True))
            for h in range(x.shape[1] // width)]


def _running_max(scr, idx, col):
    new = jnp.maximum(scr[idx][0:1, 0:1], jnp.max(col, axis=0, keepdims=True))
    scr[idx] = jnp.broadcast_to(new, scr.shape[1:])
    return new


def _value_tile_t(v_t):
    extra = jnp.where(_iota((VT_ROWS - HEAD_DIM, v_t.shape[1]), 0) == 0, 1.0, 0.0)
    return jnp.concatenate([v_t, extra], axis=0).astype(BF16)


def _rms_row_groups(x_t, width):
    parts = []
    for h in range(x_t.shape[0] // width):
        g = x_t[h * width:(h + 1) * width]
        parts.append(g * lax.rsqrt(jnp.mean(g * g, axis=0, keepdims=True) + NORM_EPS))
    return jnp.concatenate(parts, axis=0)


def _rope_t(x_t, cos_t, sin_t):
    W = x_t.shape[0]
    row = _iota(x_t.shape, 0) % HEAD_DIM
    up = pltpu.roll(x_t, W - ROT_HALF, 0)
    dn = pltpu.roll(x_t, ROT_HALF, 0)
    return x_t * cos_t + jnp.where(row < ROT_HALF, up, dn) * sin_t


def _row_group_norms(x_t, width):
    return [jnp.sqrt(jnp.sum(jnp.square(x_t[h * width:(h + 1) * width]), axis=0, keepdims=True))
            for h in range(x_t.shape[0] // width)]


def _top_k_mask_t(vals, k, rank_limit=None):
    row = _iota(vals.shape, 0)
    sel = jnp.zeros(vals.shape, jnp.int32)
    for r in range(k):
        mx = jnp.max(vals, axis=0, keepdims=True)
        idx = jnp.min(jnp.where(vals == mx, row, LANES), axis=0, keepdims=True)
        pick = row == idx
        mark = 1 if rank_limit is None else jnp.where(rank_limit > r, 1, 0)
        sel = jnp.where(pick, mark, sel)
        vals = jnp.where(pick, -jnp.inf, vals)
    return sel


def _proj_kernel(x_ref, g_ref, w_ref, o_ref, w_scr):
    @pl.when(pl.program_id(0) == 0)
    def _():
        dst = 0
        for src, width in W_IN_SECTIONS:
            w_scr[:, dst:dst + width] = w_ref[0, :, src:src + width].astype(BF16)
            dst += width
        w_scr[:, dst:] = jnp.zeros((w_scr.shape[0], w_scr.shape[1] - dst), BF16)

    x = x_ref[...]
    h = x * lax.rsqrt(jnp.mean(x * x, axis=-1, keepdims=True) + NORM_EPS) * g_ref[...]
    o_ref[...] = jnp.dot(h.astype(BF16), w_scr[...], preferred_element_type=F32)


def _proj(x2d, g, w_all, layer, tm=256):
    M, D = x2d.shape
    N = P_W
    assert w_all.shape[2] == sum(width for _, width in W_IN_SECTIONS)
    return pl.pallas_call(
        _proj_kernel,
        grid=(M // tm,),
        in_specs=[pl.BlockSpec((tm, D), lambda i: (i, 0)), _const_spec((1, D)),
                  pl.BlockSpec((1, D, w_all.shape[2]), lambda i: (layer, 0, 0), pipeline_mode=pl.Buffered(1))],
        out_specs=pl.BlockSpec((tm, N), lambda i: (i, 0)),
        out_shape=jax.ShapeDtypeStruct((M, N), F32),
        scratch_shapes=[pltpu.VMEM((D, N), BF16)],
        compiler_params=_cparams(1),
        name="in_proj",
    )(x2d, g, w_all)


def _mlstm_kernel(qk_ref, v_ref, o_ref, sm_ref, cw_ref, b_ref, g_ref, out_ref, xbuf, c_scr, m_scr, *, Lc):
    W2 = 2 * M_WIDTH
    D = M_HEAD_DIM

    @pl.when(pl.program_id(1) == 0)
    def _():
        xbuf[0:8, :] = jnp.zeros((8, W2), F32)
        c_scr[...] = jnp.zeros_like(c_scr)
        m_scr[...] = jnp.zeros_like(m_scr)

    x = qk_ref[0]
    xbuf[8:8 + Lc, :] = x
    y = cw_ref[0:1, :] * xbuf[8 - CONV_W + 1:8 - CONV_W + 1 + Lc, :]
    for j in range(1, CONV_W):
        y = y + cw_ref[j:j + 1, :] * xbuf[8 - CONV_W + 1 + j:8 - CONV_W + 1 + j + Lc, :]
    xbuf[0:8, :] = x[Lc - 8:Lc, :]
    qk = y * jax.nn.sigmoid(y)

    gi = sm_ref[0] + b_ref[...]
    lane = _iota((Lc, LANES), 1)
    lsig = jnp.minimum(gi, 0.0) - jnp.log1p(jnp.exp(-jnp.abs(gi)))
    gates = jnp.where(lane < SM_F, gi, jnp.where(lane < SM_F + M_HEADS, lsig, 0.0))
    gates_t = gates.T
    ri, ci = _iota((Lc, Lc), 0), _iota((Lc, Lc), 1)
    causal = ci <= ri
    b_col = jnp.dot(causal.astype(F32), gates, precision=_HI, preferred_element_type=F32)
    b_row = jnp.dot(gates_t[0:8, :], (ri <= ci).astype(F32), precision=_HI, preferred_element_type=F32)
    ones_col = jnp.where(lane == 0, 1.0, 0.0)

    for h in range(M_HEADS):
        q = qk[:, h * D:(h + 1) * D]
        k = qk[:, M_WIDTH + h * D:M_WIDTH + (h + 1) * D] * (D ** -0.5)
        v_aug = jnp.concatenate([v_ref[0, :, h * D:(h + 1) * D], ones_col], axis=1).astype(BF16)
        bc = b_col[:, SM_F + h:SM_F + h + 1]
        br = b_row[SM_F + h:SM_F + h + 1, :]
        li_r = gates_t[SM_I + h:SM_I + h + 1, :]
        li_c = gates[:, SM_I + h:SM_I + h + 1]
        m_prev = m_scr[h][0:1, 0:1]

        dmat = jnp.where(causal, bc - br + li_r, -jnp.inf)
        inter = bc + m_prev
        m_t = jnp.maximum(inter, jnp.max(dmat, axis=-1, keepdims=True))
        w_intra = jnp.exp(dmat - m_t)
        w_prev = jnp.exp(inter - m_t)
        qb = q.astype(BF16)
        s = lax.dot_general(qb, k.astype(BF16), _NT, preferred_element_type=F32) * w_intra
        tot = (jnp.dot(s.astype(BF16), v_aug, preferred_element_type=F32)
               + w_prev * jnp.dot(qb, c_scr[h].astype(BF16), preferred_element_type=F32))
        den = tot[:, D:D + 1]
        hh = tot[:, :D] / jnp.maximum(jnp.abs(den), jnp.exp(-m_t))

        b_last = bc[Lc - 1:Lc, :]
        g_c = b_last - bc + li_c
        m_new = jnp.maximum(b_last + m_prev, jnp.max(g_c, axis=0, keepdims=True))
        a = jnp.exp(b_last + m_prev - m_new)
        kw_t = (k * jnp.exp(g_c - m_new)).T.astype(BF16)
        c_scr[h] = a * c_scr[h] + jnp.dot(kw_t, v_aug, preferred_element_type=F32)
        m_scr[h] = jnp.broadcast_to(m_new, (8, LANES))

        hn = hh * lax.rsqrt(jnp.mean(hh * hh, axis=-1, keepdims=True) + NORM_EPS) * g_ref[0:1, h * D:(h + 1) * D]
        out_ref[0, :, h * D:(h + 1) * D] = (jax.nn.sigmoid(o_ref[0, :, h * D:(h + 1) * D]) * hn).astype(BF16)


def _mlstm(p3, conv_w, b_sm, m_norm, Lc=256):
    B, S, _ = p3.shape
    Lc = min(Lc, S)
    kern = functools.partial(_mlstm_kernel, Lc=Lc)
    return pl.pallas_call(
        kern,
        grid=(B, S // Lc),
        in_specs=[
            pl.BlockSpec((1, Lc, 2 * M_WIDTH), lambda b, c: (b, c, COL_MQK // (2 * M_WIDTH))),
            pl.BlockSpec((1, Lc, M_WIDTH), lambda b, c: (b, c, COL_MV // M_WIDTH)),
            pl.BlockSpec((1, Lc, M_WIDTH), lambda b, c: (b, c, COL_MO // M_WIDTH)),
            pl.BlockSpec((1, Lc, LANES), lambda b, c: (b, c, COL_SM // LANES)),
            _const_spec((CONV_W, 2 * M_WIDTH)),
            _const_spec((1, LANES)),
            _const_spec((1, M_WIDTH)),
        ],
        out_specs=pl.BlockSpec((1, Lc, M_WIDTH), lambda b, c: (b, c, 0)),
        out_shape=jax.ShapeDtypeStruct((B, S, M_WIDTH), BF16),
        scratch_shapes=[
            pltpu.VMEM((Lc + 8, 2 * M_WIDTH), F32),
            pltpu.VMEM((M_HEADS, M_HEAD_DIM, 2 * M_HEAD_DIM), F32),
            pltpu.VMEM((M_HEADS, 8, LANES), F32),
        ],
        compiler_params=_cparams(2),
        name="mlstm",
    )(p3, p3, p3, p3, conv_w, b_sm, m_norm)


def _moba_prep_kernel(q_ref, k_ref, v_ref, cos_ref, sin_ref, cost_ref, sint_ref, qg_ref, kg_ref,
                      qt_ref, kp_ref, vt_ref, kmean_scr, kmax_scr):
    i = pl.program_id(1)
    T = q_ref.shape[1]

    @pl.when(i == 0)
    def _():
        kmean_scr[...] = jnp.zeros_like(kmean_scr)
        kmax_scr[...] = jnp.zeros_like(kmax_scr)

    kn = _rope(_rms_groups(k_ref[0], HEAD_DIM) * kg_ref[...], cos_ref[...], sin_ref[...])
    kb = kn.astype(BF16).astype(F32)
    k_norms = _group_norms(kb, HEAD_DIM)
    qn_t = _rope_t(_rms_row_groups(q_ref[0].T, HEAD_DIM) * qg_ref[...], cost_ref[...], sint_ref[...])
    q2_t = (qn_t * SCALE_LOG2).astype(BF16).astype(F32)
    q_norms = _row_group_norms(q2_t, HEAD_DIM)
    v_t = v_ref[0].T
    row_w = _iota((ATT_WIDTH, T), 0)
    row = _iota((LANES, T), 0)
    zeros_half = jnp.zeros((HEAD_DIM, T), F32)
    kmeans = kmean_scr[...]
    for h in range(ATT_HEADS):
        qm = jnp.where((row_w >= h * HEAD_DIM) & (row_w < (h + 1) * HEAD_DIM), qn_t, 0.0)
        gs = jnp.dot(kmeans, qm, precision=_HI, preferred_element_type=F32)
        sel = _top_k_mask_t(jnp.where(row < i, gs, NEG), MOBA_TOPK, rank_limit=i)
        bound = q_norms[h] * (_running_max(kmax_scr, h, k_norms[h]) * BOUND_SLACK)
        qt_ref[0, h, 0:LANES, :] = (jnp.where((sel > 0) | (row == i), 0.0, NEG) - bound).astype(BF16)
        q_h = q2_t[h * HEAD_DIM:(h + 1) * HEAD_DIM]
        qt_ref[0, h, LANES:AUG_W, :] = jnp.concatenate([zeros_half, q_h] if h % 2 else [q_h, zeros_half],
                                                       axis=0).astype(BF16)
        vt_ref[0, h, 0] = _value_tile_t(v_t[h * HEAD_DIM:(h + 1) * HEAD_DIM, :])
    for j in range(ATT_HEADS // 2):
        kp_ref[0, j] = kb[:, j * LANES:(j + 1) * LANES].astype(BF16)
    kmean_scr[pl.ds(i, 1), :] = jnp.mean(kn, axis=0, keepdims=True)


def _moba_prep(p3, tables, qg_t, kg):
    B, S, _ = p3.shape
    T = MOBA_BLOCK
    blk = lambda col: pl.BlockSpec((1, T, ATT_WIDTH), lambda b, i: (b, i, col // ATT_WIDTH))
    tab = pl.BlockSpec((T, ATT_WIDTH), lambda b, i: (i, 0))
    tab_t = pl.BlockSpec((ATT_WIDTH, T), lambda b, i: (0, i))
    return pl.pallas_call(
        _moba_prep_kernel,
        grid=(B, S // T),
        in_specs=[blk(COL_BQ), blk(COL_BK), blk(COL_BV), tab, tab, tab_t, tab_t,
                  _const_spec((ATT_WIDTH, T)), _const_spec((1, ATT_WIDTH))],
        out_specs=[pl.BlockSpec((1, ATT_HEADS, AUG_W, T), lambda b, i: (b, 0, 0, i)),
                   pl.BlockSpec((1, ATT_HEADS // 2, T, LANES), lambda b, i: (b, 0, i, 0)),
                   pl.BlockSpec((1, ATT_HEADS, 1, VT_ROWS, T), lambda b, i: (b, 0, i, 0, 0))],
        out_shape=[jax.ShapeDtypeStruct((B, ATT_HEADS, AUG_W, S), BF16),
                   jax.ShapeDtypeStruct((B, ATT_HEADS // 2, S, LANES), BF16),
                   jax.ShapeDtypeStruct((B, ATT_HEADS, S // T, VT_ROWS, T), BF16)],
        scratch_shapes=[pltpu.VMEM((LANES, ATT_WIDTH), F32), pltpu.VMEM((ATT_HEADS, 8, LANES), F32)],
        compiler_params=_cparams(2),
        name="moba_prep",
    )(p3, p3, p3, *tables, qg_t, kg)


def _attend(tile_scores, tile_pv, lo, hi, acc_scr, m_scr):
    n_chain = acc_scr.shape[0]

    def run(exact):
        acc_scr[...] = jnp.zeros_like(acc_scr)
        if exact:
            m_scr[...] = jnp.full(m_scr.shape, NEG, F32)

        def tile(j, last):
            for c in range(n_chain):
                s_t = tile_scores(c, j, last)
                if exact:
                    m_prev = m_scr[c]
                    m_new = jnp.maximum(m_prev, jnp.max(s_t, axis=0, keepdims=True))
                    acc_scr[c] = (acc_scr[c] * jnp.exp2(m_prev - m_new)
                                  + tile_pv(c, j, jnp.exp2(s_t - m_new).astype(BF16)))
                    m_scr[c] = m_new
                else:
                    acc_scr[c] += tile_pv(c, j, jnp.exp2(s_t).astype(BF16))

        unroll = 1 if exact else ATTEND_UNROLL
        n = hi - lo

        def group(i, carry):
            for u in range(unroll):
                tile(lo + unroll * i + u, False)
            return carry

        lax.fori_loop(0, n // unroll, group, 0)
        start = lo + (n // unroll) * unroll
        b = unroll // 2
        while b:
            take = (n & b) != 0

            @pl.when(take)
            def _(start=start, b=b):
                for u in range(b):
                    tile(start + u, False)

            start = start + jnp.where(take, b, 0)
            b //= 2

        tile(hi, True)

    run(False)
    l_min = jnp.min(acc_scr[:, HEAD_DIM:HEAD_DIM + 1, :])

    @pl.when(jnp.logical_not(l_min >= L_TINY))
    def _():
        run(True)


def _head_pair_rows(o_a, o_b):
    return jnp.concatenate([o_a, o_b], axis=0).T


def _normalised_heads(acc_scr, T):
    heads = []
    for c in range(acc_scr.shape[0]):
        o = acc_scr[c, 0:HEAD_DIM, :] / acc_scr[c, HEAD_DIM:HEAD_DIM + 1, :]
        heads += [o[:, h * T:(h + 1) * T] for h in range(o.shape[1] // T)]
    return heads


def _flash_kernel(qt_ref, k_ref, vt_ref, o_ref, acc_scr, m_scr, *, T, mask_block):
    qi = pl.program_id(1)
    P, G, VH = qt_ref.shape[1], k_ref.shape[1], vt_ref.shape[1]
    hc = P // G
    N = hc * T
    q_cat = [jnp.concatenate([qt_ref[0, c * hc + h] for h in range(hc)], axis=1) for c in range(G)]
    key, qry = _iota((T, N), 0), _iota((T, N), 1) % T
    lane, row = _iota((T, LANES), 1), _iota((T, LANES), 0)

    def tile_scores(c, j, last):
        onehot = jnp.where(lane == (j * T + row) // mask_block, 1.0, 0.0).astype(BF16)
        kt = jnp.concatenate([onehot, k_ref[0, c, pl.ds(pl.multiple_of(j * T, T), T), :]], axis=1)
        s_t = jnp.dot(kt, q_cat[c], preferred_element_type=F32)
        return jnp.where(key <= qry, s_t, NEG) if last else s_t

    def tile_pv(c, j, p):
        if VH == G:
            return jnp.dot(vt_ref[0, c, j], p, preferred_element_type=F32)
        return jnp.concatenate([jnp.dot(vt_ref[0, c * hc + h, j], p[:, h * T:(h + 1) * T],
                                        preferred_element_type=F32) for h in range(hc)], axis=1)

    _attend(tile_scores, tile_pv, 0, qi, acc_scr, m_scr)
    heads = _normalised_heads(acc_scr, T)
    for j in range(P // 2):
        o_ref[0, :, j * LANES:(j + 1) * LANES] = _head_pair_rows(heads[2 * j], heads[2 * j + 1]).astype(o_ref.dtype)


def _flash(q_t, k, v_t, out_dtype, mask_block, T=256):
    B, P, _, S = q_t.shape
    T = min(T, S)
    KH, VH = k.shape[1], v_t.shape[1]
    assert v_t.shape == (B, VH, S // T, VT_ROWS, T) and k.shape == (B, KH, S, LANES)
    return pl.pallas_call(
        functools.partial(_flash_kernel, T=T, mask_block=mask_block),
        grid=(B, S // T),
        in_specs=[pl.BlockSpec((1, P, AUG_W, T), lambda b, i: (b, 0, 0, i)),
                  pl.BlockSpec((1, KH, S, LANES), lambda b, i: (b, 0, 0, 0)),
                  pl.BlockSpec((1, VH, S // T, VT_ROWS, T), lambda b, i: (b, 0, 0, 0, 0))],
        out_specs=pl.BlockSpec((1, T, P * HEAD_DIM), lambda b, i: (b, i, 0)),
        out_shape=jax.ShapeDtypeStruct((B, S, P * HEAD_DIM), out_dtype),
        scratch_shapes=[pltpu.VMEM((KH, VT_ROWS, P // KH * T), F32), pltpu.VMEM((KH, 1, P // KH * T), F32)],
        compiler_params=_cparams(2),
        name="flash",
    )(q_t, k, v_t)


def _window_kernel(qt_ref, k_ref, vt_ref, oc_ref, os_ref, sm_ref, y_ref, acc_scr, m_scr, *, T):
    R = ATT_HEADS
    qi = pl.program_id(1)
    q_cat = jnp.concatenate([qt_ref[0, h, LANES:AUG_W, :] for h in range(R)], axis=1)
    tq = qi * T + _iota((T, R * T), 1) % T
    row = _iota((T, R * T), 0)

    def tile_scores(c, j, last):
        kt = k_ref[0, pl.ds(pl.multiple_of(j * T, T), T), :]
        s_t = jnp.dot(kt, q_cat, preferred_element_type=F32)
        key = j * T + row
        return jnp.where((key <= tq) & (key > tq - WINDOW), s_t, NEG)

    def tile_pv(c, j, p):
        return jnp.dot(vt_ref[0, j], p, preferred_element_type=F32)

    _attend(tile_scores, tile_pv, jnp.maximum(qi - (WINDOW + T - 1) // T, 0), qi, acc_scr, m_scr)
    sig = jax.nn.sigmoid(sm_ref[0])
    lane = _iota((T, LANES), 1)
    heads = _normalised_heads(acc_scr, T)
    for j in range(R // 2):
        o_w = _head_pair_rows(heads[2 * j], heads[2 * j + 1])

        def gate(branch):
            c = SM_G + branch * R + 2 * j
            return jnp.where(lane < HEAD_DIM, sig[:, c:c + 1], sig[:, c + 1:c + 2])
        sl = slice(j * LANES, (j + 1) * LANES)
        y_ref[0, :, sl] = (gate(0) * oc_ref[0, :, sl] + gate(1) * os_ref[0, :, sl] + gate(2) * o_w).astype(y_ref.dtype)


def _window_combine(q_t, kw, vw_t, o_c, o_s, p3, T=256):
    B, R, _, S = q_t.shape
    T = min(T, S)
    packed = pl.BlockSpec((1, T, ATT_WIDTH), lambda b, i: (b, i, 0))
    return pl.pallas_call(
        functools.partial(_window_kernel, T=T),
        grid=(B, S // T),
        in_specs=[pl.BlockSpec((1, R, AUG_W, T), lambda b, i: (b, 0, 0, i)),
                  pl.BlockSpec((1, S, LANES), lambda b, i: (b, 0, 0)),
                  pl.BlockSpec((1, S // T, VT_ROWS, T), lambda b, i: (b, 0, 0, 0)),
                  packed, packed,
                  pl.BlockSpec((1, T, LANES), lambda b, i: (b, i, COL_SM // LANES))],
        out_specs=packed,
        out_shape=jax.ShapeDtypeStruct((B, S, ATT_WIDTH), BF16),
        scratch_shapes=[pltpu.VMEM((1, VT_ROWS, R * T), F32), pltpu.VMEM((1, 1, R * T), F32)],
        compiler_params=_cparams(2),
        name="nsa_window",
    )(q_t, kw, vw_t, o_c, o_s, p3)


def _compress_kernel(t_ref, pea_ref, peb_ref, w1a_ref, w1b_ref, w2_ref, kg_ref, kc_ref, vct_ref, t_scr):
    n = t_scr.shape[0]
    for r in range(CMP_STRIDE):
        t_scr[:, r * LANES:(r + 1) * LANES] = t_ref[0, pl.ds(r, n, stride=CMP_STRIDE), :]
    t = t_scr[...]
    a = jnp.dot((t + pea_ref[...]).astype(BF16), w1a_ref[...], preferred_element_type=F32)
    b = jnp.dot((t + peb_ref[...]).astype(BF16), w1b_ref[...], preferred_element_type=F32)
    hid = a + pltpu.roll(b, n - 1, 0)
    hid = hid * jax.nn.sigmoid(hid)
    kv = jnp.dot(hid.astype(BF16), w2_ref[...], preferred_element_type=F32)
    lane = _iota(kv.shape, 1)
    ms = jnp.sum(jnp.where(lane < HEAD_DIM, kv * kv, 0.0), axis=-1, keepdims=True) * (1.0 / HEAD_DIM)
    kc_ref[0] = jnp.where(lane < HEAD_DIM, kv * lax.rsqrt(ms + NORM_EPS) * kg_ref[...], 0.0).astype(kc_ref.dtype)
    vct_ref[0] = kv.T[HEAD_DIM:, :].astype(vct_ref.dtype)


def _compress(p3, pea, peb, w1a, w1b, w2, kg):
    B, S, _ = p3.shape
    n, W = S // CMP_STRIDE, CMP_STRIDE * LANES
    return pl.pallas_call(
        _compress_kernel,
        grid=(B,),
        in_specs=[pl.BlockSpec((1, S, LANES), lambda b: (b, 0, COL_NKV // LANES)),
                  _const_spec((1, W)), _const_spec((1, W)),
                  _const_spec(w1a.shape), _const_spec(w1b.shape), _const_spec(w2.shape),
                  _const_spec((1, LANES))],
        out_specs=[pl.BlockSpec((1, n, LANES), lambda b: (b, 0, 0)),
                   pl.BlockSpec((1, HEAD_DIM, n), lambda b: (b, 0, 0))],
        out_shape=[jax.ShapeDtypeStruct((B, n, LANES), BF16), jax.ShapeDtypeStruct((B, HEAD_DIM, n), BF16)],
        scratch_shapes=[pltpu.VMEM((n, W), F32)],
        compiler_params=_cparams(1),
        name="nsa_compress",
    )(p3, pea, peb, w1a, w1b, w2, kg)


def _nsa_prep_kernel(q_ref, kv_ref, cos_ref, sin_ref, cost_ref, sint_ref, qg_ref, kg_ref, kc_ref, vct_ref, ovt_ref,
                     oc_ref, qt_ref, ks_ref, kw_ref, vst_ref, vwt_ref, kmax_scr, *, n_sel):
    i = pl.program_id(1)
    T = q_ref.shape[1]
    nc = kc_ref.shape[1]

    @pl.when(i == 0)
    def _():
        kmax_scr[...] = jnp.zeros_like(kmax_scr)

    cos, sin = cos_ref[...], sin_ref[...]
    lane = _iota((T, LANES), 1)
    row = _iota((LANES, T), 0)
    tq = i * T + _iota((1, T), 1)

    qn_t = _rms_row_groups(q_ref[0].T, HEAD_DIM) * qg_ref[...]
    qr_t = _rope_t(qn_t, cost_ref[...], sint_ref[...])
    zeros_half = jnp.zeros((HEAD_DIM, T), F32)

    kc, vc_t = kc_ref[0], vct_ref[0]
    valid = _iota((nc, T), 0) * CMP_STRIDE + (CMP_LEN - 1) <= tq
    psum = jnp.zeros((nc, T), F32)
    o_heads = []
    for h in range(ATT_HEADS):
        qh = jnp.concatenate([qn_t[h * HEAD_DIM:(h + 1) * HEAD_DIM] * SCALE, zeros_half], axis=0).astype(BF16)
        s = jnp.where(valid, jnp.dot(kc, qh, preferred_element_type=F32), NEG)
        e = jnp.exp(s - jnp.max(s, axis=0, keepdims=True))
        p = jnp.where(valid, e / jnp.sum(e, axis=0, keepdims=True), 0.0)
        o_heads.append(jnp.dot(vc_t, p.astype(BF16), preferred_element_type=F32))
        psum = psum + p
    for j in range(ATT_HEADS // 2):
        oc_ref[0, :, j * LANES:(j + 1) * LANES] = _head_pair_rows(o_heads[2 * j], o_heads[2 * j + 1])

    imp = jnp.dot(ovt_ref[...], psum, precision=_HI, preferred_element_type=F32)
    blk_q = tq // SEL_BLOCK
    causal_blk = row <= blk_q
    forced = causal_blk & ((row == 0) | (row >= blk_q - 1))
    imp = jnp.where(forced, BIG, jnp.where(causal_blk, imp, NEG))
    imp = jnp.where(row < n_sel, imp, -jnp.inf)
    sel = _top_k_mask_t(imp, min(SEL_TOPK, n_sel))
    bias = jnp.where(sel > 0, 0.0, NEG)

    cos_k = jnp.where(lane < HEAD_DIM, cos[:, :LANES], 1.0)
    sin_k = jnp.where(lane < HEAD_DIM, sin[:, :LANES], 0.0)

    def key_pair(x, gain, idx):
        ms = jnp.sum(jnp.where(lane < HEAD_DIM, x * x, 0.0), axis=-1, keepdims=True) * (1.0 / HEAD_DIM)
        kn = _rope(x * lax.rsqrt(ms + NORM_EPS) * gain, cos_k, sin_k)
        kb = jnp.where(lane < HEAD_DIM, kn, 0.0).astype(BF16).astype(F32)
        return kb, _value_tile_t(x.T[HEAD_DIM:, :]), _running_max(kmax_scr, idx, _group_norms(kb, LANES)[0])

    ks, vst_ref[0, 0], ks_max = key_pair(kv_ref[0, :, LANES:2 * LANES], kg_ref[0:1, :], 0)
    kw, vwt_ref[0, 0], kw_max = key_pair(kv_ref[0, :, 2 * LANES:3 * LANES], kg_ref[1:2, :], 1)
    ks_ref[0] = ks.astype(BF16)
    kw_ref[0] = jnp.where(lane == HEAD_DIM, 1.0, kw).astype(BF16)

    q2_t = (qr_t * SCALE_LOG2).astype(BF16).astype(F32)
    row_h = _iota((HEAD_DIM, T), 0)
    for h, q_norm in enumerate(_row_group_norms(q2_t, HEAD_DIM)):
        qt_ref[0, h, 0:LANES, :] = (bias - q_norm * (ks_max * BOUND_SLACK)).astype(BF16)
        shift_rows = jnp.where(row_h == 0, -q_norm * (kw_max * BOUND_SLACK), 0.0)
        qt_ref[0, h, LANES:AUG_W, :] = jnp.concatenate([q2_t[h * HEAD_DIM:(h + 1) * HEAD_DIM], shift_rows],
                                                       axis=0).astype(BF16)


def _nsa_prep(p3, tables, qg_t, kg2, kc, vc_t, overlap_t, T=256):
    B, S, _ = p3.shape
    T = min(T, S)
    nc = kc.shape[1]
    n_sel = S // SEL_BLOCK
    tab = pl.BlockSpec((T, ATT_WIDTH), lambda b, i: (i, 0))
    tab_t = pl.BlockSpec((ATT_WIDTH, T), lambda b, i: (0, i))
    v_t = pl.BlockSpec((1, 1, VT_ROWS, T), lambda b, i: (b, i, 0, 0))
    v_t_shape = jax.ShapeDtypeStruct((B, S // T, VT_ROWS, T), BF16)
    return pl.pallas_call(
        functools.partial(_nsa_prep_kernel, n_sel=n_sel),
        grid=(B, S // T),
        in_specs=[pl.BlockSpec((1, T, ATT_WIDTH), lambda b, i: (b, i, COL_NQ // ATT_WIDTH)),
                  pl.BlockSpec((1, T, 3 * LANES), lambda b, i: (b, i, COL_NKV // (3 * LANES))),
                  tab, tab, tab_t, tab_t,
                  _const_spec((ATT_WIDTH, T)), _const_spec((2, LANES)),
                  pl.BlockSpec((1, nc, LANES), lambda b, i: (b, 0, 0)),
                  pl.BlockSpec((1, HEAD_DIM, nc), lambda b, i: (b, 0, 0)),
                  _const_spec((LANES, nc))],
        out_specs=[pl.BlockSpec((1, T, ATT_WIDTH), lambda b, i: (b, i, 0)),
                   pl.BlockSpec((1, ATT_HEADS, AUG_W, T), lambda b, i: (b, 0, 0, i)),
                   pl.BlockSpec((1, T, LANES), lambda b, i: (b, i, 0)),
                   pl.BlockSpec((1, T, LANES), lambda b, i: (b, i, 0)),
                   v_t, v_t],
        out_shape=[jax.ShapeDtypeStruct((B, S, ATT_WIDTH), F32),
                   jax.ShapeDtypeStruct((B, ATT_HEADS, AUG_W, S), BF16),
                   jax.ShapeDtypeStruct((B, S, LANES), BF16),
                   jax.ShapeDtypeStruct((B, S, LANES), BF16),
                   v_t_shape, v_t_shape],
        scratch_shapes=[pltpu.VMEM((2, 8, LANES), F32)],
        compiler_params=_cparams(2),
        name="nsa_prep",
    )(p3, p3, *tables, qg_t, kg2, kc, vc_t, overlap_t)


def _out_ffn_kernel(x_ref, ym_ref, yb_ref, yn_ref, wo_ref, g_ref, w1_ref, w2_ref, o_ref, *, fc):
    mix = jnp.concatenate([ym_ref[...], yb_ref[...], yn_ref[...]], axis=1)
    x = x_ref[...] + jnp.dot(mix, wo_ref[...], preferred_element_type=F32)
    hb = (x * lax.rsqrt(jnp.mean(x * x, axis=-1, keepdims=True) + NORM_EPS) * g_ref[...]).astype(BF16)
    acc = x
    for c in range(w1_ref.shape[1] // fc):
        u = jnp.maximum(jnp.dot(hb, w1_ref[:, c * fc:(c + 1) * fc], preferred_element_type=F32), 0.0)
        acc = acc + jnp.dot((u * u).astype(BF16), w2_ref[c * fc:(c + 1) * fc, :], preferred_element_type=F32)
    o_ref[...] = acc


def _out_ffn(x2d, y_m, y_b, y_n, wo, g, w1, w2, tm=256, fc=1024):
    M, D = x2d.shape
    tm = min(tm, M)
    rows = lambda w: pl.BlockSpec((tm, w), lambda i: (i, 0))
    return pl.pallas_call(
        functools.partial(_out_ffn_kernel, fc=fc),
        grid=(M // tm,),
        in_specs=[rows(D), rows(M_WIDTH), rows(ATT_WIDTH), rows(ATT_WIDTH),
                  _const_spec(wo.shape), _const_spec((1, D)), _const_spec(w1.shape), _const_spec(w2.shape)],
        out_specs=rows(D),
        out_shape=jax.ShapeDtypeStruct((M, D), F32),
        compiler_params=_cparams(1),
        name="out_ffn",
    )(x2d, y_m, y_b, y_n, wo, g, w1, w2)


def _rope_tables(S):
    inv_freq = jnp.exp(-math.log(ROPE_THETA) * jnp.arange(ROT_HALF, dtype=F32) * (2.0 / ROT_DIM))
    ang = jnp.arange(S, dtype=F32)[:, None] * inv_freq[None, :]
    cos, sin = jnp.cos(ang), jnp.sin(ang)
    rest = HEAD_DIM - ROT_DIM
    cos64 = jnp.concatenate([cos, cos, jnp.ones((S, rest), F32)], axis=1)
    sin64 = jnp.concatenate([-sin, sin, jnp.zeros((S, rest), F32)], axis=1)
    return jnp.tile(cos64, (1, ATT_HEADS)), jnp.tile(sin64, (1, ATT_HEADS))


def _overlap_matrix(nc):
    c_start = np.arange(nc)[:, None] * CMP_STRIDE
    s_start = np.arange(LANES)[None, :] * SEL_BLOCK
    return jnp.asarray(((c_start < s_start + SEL_BLOCK) & (c_start + CMP_LEN > s_start)).astype(np.float32))


def _compress_weights(pe, w1, w2):
    half = CMP_LEN // 2
    pe_r = jnp.concatenate([pe[0], pe[1]], axis=-1)
    pea = pe_r[:half].reshape(1, half * LANES)
    peb = pe_r[half:].reshape(1, half * LANES)
    w1r = w1.reshape(2, CMP_LEN, HEAD_DIM, CMP_HIDDEN)
    z = jnp.zeros_like(w1r[0])
    wk = jnp.concatenate([w1r[0], z], axis=-1)
    wv = jnp.concatenate([z, w1r[1]], axis=-1)
    wboth = jnp.concatenate([wk, wv], axis=1)
    w1a = wboth[:half].reshape(half * LANES, 2 * CMP_HIDDEN).astype(BF16)
    w1b = wboth[half:].reshape(half * LANES, 2 * CMP_HIDDEN).astype(BF16)
    z2 = jnp.zeros_like(w2[0])
    w2bd = jnp.concatenate([jnp.concatenate([w2[0], z2], axis=1),
                            jnp.concatenate([z2, w2[1]], axis=1)], axis=0).astype(BF16)
    return pea, peb, w1a, w1b, w2bd


def _pad_lanes(v, width=LANES):
    return jnp.concatenate([v, jnp.zeros((width - v.shape[0],), v.dtype)])[None, :]


def kernel(x, w_in, b_if, conv_qk, m_norm, moba_qk_norm, nsa_q_norm, nsa_k_norm, cmp_pe, cmp_w1, cmp_w2,
           w_out, norm_mix, norm_ffn, w_ff1, w_ff2):
    B, S, D = x.shape
    depth = w_in.shape[0]
    cos, sin = _rope_tables(S)
    tables = (cos, sin, cos.T, sin.T)
    overlap_t = _overlap_matrix(S // CMP_STRIDE).T
    x2d = x.reshape(B * S, D)
    for l in range(depth):
        p3 = _proj(x2d, norm_mix[l][None, :], w_in, l).reshape(B, S, P_W)

        y_m = _mlstm(p3, conv_qk[l], _pad_lanes(b_if[l]), m_norm[l][None, :])

        tile_g = lambda g: jnp.tile(g, ATT_HEADS)[None, :]
        rows_g = lambda g: jnp.broadcast_to(jnp.tile(g, ATT_HEADS)[:, None], (ATT_WIDTH, min(MOBA_BLOCK, S)))
        qt_b, kp_b, vt_b = _moba_prep(p3, tables, rows_g(moba_qk_norm[l, 0]), tile_g(moba_qk_norm[l, 1]))
        y_b = _flash(qt_b, kp_b, vt_b, BF16, MOBA_BLOCK)

        pea, peb, w1a, w1b, w2bd = _compress_weights(cmp_pe[l], cmp_w1[l], cmp_w2[l])
        kc, vc_t = _compress(p3, pea, peb, w1a, w1b, w2bd, _pad_lanes(nsa_k_norm[l, 0]))
        kg2 = jnp.concatenate([_pad_lanes(nsa_k_norm[l, 1]), _pad_lanes(nsa_k_norm[l, 2])], axis=0)
        o_c, qt_n, ks, kw, vs_t, vw_t = _nsa_prep(p3, tables, rows_g(nsa_q_norm[l]), kg2, kc, vc_t, overlap_t)
        o_s = _flash(qt_n, ks[:, None], vs_t[:, None], F32, SEL_BLOCK)
        y_n = _window_combine(qt_n, kw, vw_t, o_c, o_s, p3)

        x2d = _out_ffn(x2d, y_m.reshape(B * S, M_WIDTH), y_b.reshape(B * S, ATT_WIDTH), y_n.reshape(B * S, ATT_WIDTH),
                       w_out[l].astype(BF16), norm_ffn[l][None, :], w_ff1[l].astype(BF16), w_ff2[l].astype(BF16))
    return x2d.reshape(B, S, D)
```

```python
import functools
import math

import jax
import jax.numpy as jnp
import numpy as np
from jax import lax
from jax.experimental import pallas as pl
from jax.experimental.pallas import tpu as pltpu

F32 = jnp.float32
BF16 = jnp.bfloat16

HEAD_DIM = 64
M_HEADS = 4
M_HEAD_DIM = 128
M_WIDTH = M_HEADS * M_HEAD_DIM
CONV_W = 4
ATT_HEADS = 4
ATT_WIDTH = ATT_HEADS * HEAD_DIM
MOBA_BLOCK = 256
MOBA_TOPK = 3
CMP_LEN = 32
CMP_STRIDE = 16
CMP_HIDDEN = 128
SEL_BLOCK = 64
SEL_TOPK = 16
WINDOW = 512
ROPE_THETA = 500000.0
ROT_DIM = HEAD_DIM // 4
ROT_HALF = ROT_DIM // 2
NORM_EPS = 1e-6
NEG = -1e30
BIG = 1e9
SCALE = HEAD_DIM ** -0.5
SCALE_LOG2 = SCALE * math.log2(math.e)
BOUND_SLACK = 1.0 + 2.0 ** -7
L_TINY = 2.0 ** -100

LANES = 128
AUG_W = 2 * LANES
VT_ROWS = HEAD_DIM + 16
ATTEND_GROUP = 4
VMEM_LIMIT = 56 * 1024 * 1024

COL_MQK, COL_MV, COL_MO = 0, 1024, 1536
COL_BQ, COL_BK, COL_BV = 2048, 2304, 2560
COL_NQ, COL_NKV, COL_SM = 2816, 3072, 3456
P_W = 3584
SM_I, SM_F, SM_G = 0, 4, 8
_W_ATT0 = 4 * M_WIDTH + 2 * M_HEADS
_W_ATT1 = _W_ATT0 + 4 * ATT_WIDTH + 6 * HEAD_DIM
W_IN_SECTIONS = ((0, 4 * M_WIDTH), (_W_ATT0, _W_ATT1 - _W_ATT0), (4 * M_WIDTH, 2 * M_HEADS), (_W_ATT1, 3 * ATT_HEADS))

_NT = (((1,), (1,)), ((), ()))
_HI = lax.Precision.HIGHEST


def _iota(shape, dim):
    return lax.broadcasted_iota(jnp.int32, shape, dim)


def _cparams(n_axes):
    return pltpu.CompilerParams(dimension_semantics=("arbitrary",) * n_axes,
                                vmem_limit_bytes=VMEM_LIMIT)


def _const_spec(shape):
    nd = len(shape)
    return pl.BlockSpec(shape, lambda *_: (0,) * nd)


def _rms_groups(x, width):
    T, W = x.shape
    x2 = x * x
    lane = _iota((T, W), 1)
    scale = None
    for h in range(W // width):
        r = lax.rsqrt(jnp.mean(x2[:, h * width:(h + 1) * width], axis=-1, keepdims=True) + NORM_EPS)
        scale = r if scale is None else jnp.where(lane >= h * width, r, scale)
    return x * scale


def _rope(x, cos, sin):
    W = x.shape[1]
    lane = _iota(x.shape, 1) % HEAD_DIM
    up = pltpu.roll(x, W - ROT_HALF, 1)
    dn = pltpu.roll(x, ROT_HALF, 1)
    return x * cos + jnp.where(lane < ROT_HALF, up, dn) * sin


def _head_piece(x, h):
    pair = x[:, (h // 2) * LANES:(h // 2 + 1) * LANES]
    if h % 2:
        pair = pltpu.roll(pair, HEAD_DIM, 1)
    return jnp.where(_iota(pair.shape, 1) < HEAD_DIM, pair, 0.0)


def _group_norms(x, width):
    x2 = x * x
    return [jnp.sqrt(jnp.sum(x2[:, h * width:(h + 1) * width], axis=-1, keepdims=True))
            for h in range(x.shape[1] // width)]


def _running_max(scr, idx, col):
    new = jnp.maximum(scr[idx][0:1, 0:1], jnp.max(col, axis=0, keepdims=True))
    scr[idx] = jnp.broadcast_to(new, scr.shape[1:])
    return new


def _value_tile_t(v_t):
    extra = jnp.where(_iota((VT_ROWS - HEAD_DIM, v_t.shape[1]), 0) == 0, 1.0, 0.0)
    return jnp.concatenate([v_t, extra], axis=0).astype(BF16)


def _rms_row_groups(x_t, width):
    parts = []
    for h in range(x_t.shape[0] // width):
        g = x_t[h * width:(h + 1) * width]
        parts.append(g * lax.rsqrt(jnp.mean(g * g, axis=0, keepdims=True) + NORM_EPS))
    return jnp.concatenate(parts, axis=0)


def _rope_t(x_t, cos_t, sin_t):
    W = x_t.shape[0]
    row = _iota(x_t.shape, 0) % HEAD_DIM
    up = pltpu.roll(x_t, W - ROT_HALF, 0)
    dn = pltpu.roll(x_t, ROT_HALF, 0)
    return x_t * cos_t + jnp.where(row < ROT_HALF, up, dn) * sin_t


def _row_group_norms(x_t, width):
    return [jnp.sqrt(jnp.sum(jnp.square(x_t[h * width:(h + 1) * width]), axis=0, keepdims=True))
            for h in range(x_t.shape[0] // width)]


def _top_k_mask_t(vals, k, rank_limit=None):
    row = _iota(vals.shape, 0)
    sel = jnp.zeros(vals.shape, jnp.int32)
    for r in range(k):
        mx = jnp.max(vals, axis=0, keepdims=True)
        idx = jnp.min(jnp.where(vals == mx, row, LANES), axis=0, keepdims=True)
        pick = row == idx
        mark = 1 if rank_limit is None else jnp.where(rank_limit > r, 1, 0)
        sel = jnp.where(pick, mark, sel)
        vals = jnp.where(pick, -jnp.inf, vals)
    return sel


def _proj_kernel(x_ref, g_ref, w_ref, o_ref, w_scr):
    @pl.when(pl.program_id(0) == 0)
    def _():
        dst = 0
        for src, width in W_IN_SECTIONS:
            w_scr[:, dst:dst + width] = w_ref[0, :, src:src + width].astype(BF16)
            dst += width
        w_scr[:, dst:] = jnp.zeros((w_scr.shape[0], w_scr.shape[1] - dst), BF16)

    x = x_ref[...]
    h = x * lax.rsqrt(jnp.mean(x * x, axis=-1, keepdims=True) + NORM_EPS) * g_ref[...]
    o_ref[...] = jnp.dot(h.astype(BF16), w_scr[...], preferred_element_type=F32)


def _proj(x2d, g, w_all, layer, tm=512):
    M, D = x2d.shape
    N = P_W
    assert w_all.shape[2] == sum(width for _, width in W_IN_SECTIONS)
    return pl.pallas_call(
        _proj_kernel,
        grid=(M // tm,),
        in_specs=[pl.BlockSpec((tm, D), lambda i: (i, 0)), _const_spec((1, D)),
                  pl.BlockSpec((1, D, w_all.shape[2]), lambda i: (layer, 0, 0), pipeline_mode=pl.Buffered(1))],
        out_specs=pl.BlockSpec((tm, N), lambda i: (i, 0)),
        out_shape=jax.ShapeDtypeStruct((M, N), F32),
        scratch_shapes=[pltpu.VMEM((D, N), BF16)],
        compiler_params=_cparams(1),
        name="in_proj",
    )(x2d, g, w_all)


def _mlstm_kernel(qk_ref, v_ref, o_ref, sm_ref, cw_ref, b_ref, g_ref, out_ref, xbuf, c_scr, m_scr, *, Lc):
    W2 = 2 * M_WIDTH
    D = M_HEAD_DIM

    @pl.when(pl.program_id(1) == 0)
    def _():
        xbuf[0:8, :] = jnp.zeros((8, W2), F32)
        c_scr[...] = jnp.zeros_like(c_scr)
        m_scr[...] = jnp.zeros_like(m_scr)

    x = qk_ref[0]
    xbuf[8:8 + Lc, :] = x
    y = cw_ref[0:1, :] * xbuf[8 - CONV_W + 1:8 - CONV_W + 1 + Lc, :]
    for j in range(1, CONV_W):
        y = y + cw_ref[j:j + 1, :] * xbuf[8 - CONV_W + 1 + j:8 - CONV_W + 1 + j + Lc, :]
    xbuf[0:8, :] = x[Lc - 8:Lc, :]
    qk = y * jax.nn.sigmoid(y)

    gi = sm_ref[0] + b_ref[...]
    lane = _iota((Lc, LANES), 1)
    lsig = jnp.minimum(gi, 0.0) - jnp.log1p(jnp.exp(-jnp.abs(gi)))
    gates = jnp.where(lane < SM_F, gi, jnp.where(lane < SM_F + M_HEADS, lsig, 0.0))
    gates_t = gates.T
    ri, ci = _iota((Lc, Lc), 0), _iota((Lc, Lc), 1)
    causal = ci <= ri
    b_col = jnp.dot(causal.astype(F32), gates, precision=_HI, preferred_element_type=F32)
    b_row = jnp.dot(gates_t[0:8, :], (ri <= ci).astype(F32), precision=_HI, preferred_element_type=F32)
    ones_col = jnp.where(lane == 0, 1.0, 0.0)

    for h in range(M_HEADS):
        q = qk[:, h * D:(h + 1) * D]
        k = qk[:, M_WIDTH + h * D:M_WIDTH + (h + 1) * D] * (D ** -0.5)
        v_aug = jnp.concatenate([v_ref[0, :, h * D:(h + 1) * D], ones_col], axis=1).astype(BF16)
        bc = b_col[:, SM_F + h:SM_F + h + 1]
        br = b_row[SM_F + h:SM_F + h + 1, :]
        li_r = gates_t[SM_I + h:SM_I + h + 1, :]
        li_c = gates[:, SM_I + h:SM_I + h + 1]
        m_prev = m_scr[h][0:1, 0:1]

        dmat = jnp.where(causal, bc - br + li_r, -jnp.inf)
        inter = bc + m_prev
        m_t = jnp.maximum(inter, jnp.max(dmat, axis=-1, keepdims=True))
        w_intra = jnp.exp(dmat - m_t)
        w_prev = jnp.exp(inter - m_t)
        qb = q.astype(BF16)
        s = lax.dot_general(qb, k.astype(BF16), _NT, preferred_element_type=F32) * w_intra
        tot = (jnp.dot(s.astype(BF16), v_aug, preferred_element_type=F32)
               + w_prev * jnp.dot(qb, c_scr[h].astype(BF16), preferred_element_type=F32))
        den = tot[:, D:D + 1]
        hh = tot[:, :D] / jnp.maximum(jnp.abs(den), jnp.exp(-m_t))

        b_last = bc[Lc - 1:Lc, :]
        g_c = b_last - bc + li_c
        m_new = jnp.maximum(b_last + m_prev, jnp.max(g_c, axis=0, keepdims=True))
        a = jnp.exp(b_last + m_prev - m_new)
        kw_t = (k * jnp.exp(g_c - m_new)).T.astype(BF16)
        c_scr[h] = a * c_scr[h] + jnp.dot(kw_t, v_aug, preferred_element_type=F32)
        m_scr[h] = jnp.broadcast_to(m_new, (8, LANES))

        hn = hh * lax.rsqrt(jnp.mean(hh * hh, axis=-1, keepdims=True) + NORM_EPS) * g_ref[0:1, h * D:(h + 1) * D]
        out_ref[0, :, h * D:(h + 1) * D] = (jax.nn.sigmoid(o_ref[0, :, h * D:(h + 1) * D]) * hn).astype(BF16)


def _mlstm(p3, conv_w, b_sm, m_norm, Lc=256):
    B, S, _ = p3.shape
    Lc = min(Lc, S)
    kern = functools.partial(_mlstm_kernel, Lc=Lc)
    return pl.pallas_call(
        kern,
        grid=(B, S // Lc),
        in_specs=[
            pl.BlockSpec((1, Lc, 2 * M_WIDTH), lambda b, c: (b, c, COL_MQK // (2 * M_WIDTH))),
            pl.BlockSpec((1, Lc, M_WIDTH), lambda b, c: (b, c, COL_MV // M_WIDTH)),
            pl.BlockSpec((1, Lc, M_WIDTH), lambda b, c: (b, c, COL_MO // M_WIDTH)),
            pl.BlockSpec((1, Lc, LANES), lambda b, c: (b, c, COL_SM // LANES)),
            _const_spec((CONV_W, 2 * M_WIDTH)),
            _const_spec((1, LANES)),
            _const_spec((1, M_WIDTH)),
        ],
        out_specs=pl.BlockSpec((1, Lc, M_WIDTH), lambda b, c: (b, c, 0)),
        out_shape=jax.ShapeDtypeStruct((B, S, M_WIDTH), BF16),
        scratch_shapes=[
            pltpu.VMEM((Lc + 8, 2 * M_WIDTH), F32),
            pltpu.VMEM((M_HEADS, M_HEAD_DIM, 2 * M_HEAD_DIM), F32),
            pltpu.VMEM((M_HEADS, 8, LANES), F32),
        ],
        compiler_params=_cparams(2),
        name="mlstm",
    )(p3, p3, p3, p3, conv_w, b_sm, m_norm)


def _moba_prep_kernel(q_ref, k_ref, v_ref, cos_ref, sin_ref, cost_ref, sint_ref, qg_ref, kg_ref,
                      qt_ref, kp_ref, vt_ref, kmean_scr, kmax_scr):
    i = pl.program_id(1)
    T = q_ref.shape[1]

    @pl.when(i == 0)
    def _():
        kmean_scr[...] = jnp.zeros_like(kmean_scr)
        kmax_scr[...] = jnp.zeros_like(kmax_scr)

    kn = _rope(_rms_groups(k_ref[0], HEAD_DIM) * kg_ref[...], cos_ref[...], sin_ref[...])
    kb = kn.astype(BF16).astype(F32)
    k_norms = _group_norms(kb, HEAD_DIM)
    qn_t = _rope_t(_rms_row_groups(q_ref[0].T, HEAD_DIM) * qg_ref[...], cost_ref[...], sint_ref[...])
    q2_t = (qn_t * SCALE_LOG2).astype(BF16).astype(F32)
    q_norms = _row_group_norms(q2_t, HEAD_DIM)
    v_t = v_ref[0].T
    row_w = _iota((ATT_WIDTH, T), 0)
    row = _iota((LANES, T), 0)
    zeros_half = jnp.zeros((HEAD_DIM, T), F32)
    kmeans = kmean_scr[...]
    for h in range(ATT_HEADS):
        qm = jnp.where((row_w >= h * HEAD_DIM) & (row_w < (h + 1) * HEAD_DIM), qn_t, 0.0)
        gs = jnp.dot(kmeans, qm, precision=_HI, preferred_element_type=F32)
        sel = _top_k_mask_t(jnp.where(row < i, gs, NEG), MOBA_TOPK, rank_limit=i)
        bound = q_norms[h] * (_running_max(kmax_scr, h, k_norms[h]) * BOUND_SLACK)
        qt_ref[0, h, 0:LANES, :] = (jnp.where((sel > 0) | (row == i), 0.0, NEG) - bound).astype(BF16)
        q_h = q2_t[h * HEAD_DIM:(h + 1) * HEAD_DIM]
        qt_ref[0, h, LANES:AUG_W, :] = jnp.concatenate([zeros_half, q_h] if h % 2 else [q_h, zeros_half],
                                                       axis=0).astype(BF16)
        vt_ref[0, h, 0] = _value_tile_t(v_t[h * HEAD_DIM:(h + 1) * HEAD_DIM, :])
    for j in range(ATT_HEADS // 2):
        kp_ref[0, j] = kb[:, j * LANES:(j + 1) * LANES].astype(BF16)
    kmean_scr[pl.ds(i, 1), :] = jnp.mean(kn, axis=0, keepdims=True)


def _moba_prep(p3, tables, qg_t, kg):
    B, S, _ = p3.shape
    T = MOBA_BLOCK
    blk = lambda col: pl.BlockSpec((1, T, ATT_WIDTH), lambda b, i: (b, i, col // ATT_WIDTH))
    tab = pl.BlockSpec((T, ATT_WIDTH), lambda b, i: (i, 0))
    tab_t = pl.BlockSpec((ATT_WIDTH, T), lambda b, i: (0, i))
    return pl.pallas_call(
        _moba_prep_kernel,
        grid=(B, S // T),
        in_specs=[blk(COL_BQ), blk(COL_BK), blk(COL_BV), tab, tab, tab_t, tab_t,
                  _const_spec((ATT_WIDTH, T)), _const_spec((1, ATT_WIDTH))],
        out_specs=[pl.BlockSpec((1, ATT_HEADS, AUG_W, T), lambda b, i: (b, 0, 0, i)),
                   pl.BlockSpec((1, ATT_HEADS // 2, T, LANES), lambda b, i: (b, 0, i, 0)),
                   pl.BlockSpec((1, ATT_HEADS, 1, VT_ROWS, T), lambda b, i: (b, 0, i, 0, 0))],
        out_shape=[jax.ShapeDtypeStruct((B, ATT_HEADS, AUG_W, S), BF16),
                   jax.ShapeDtypeStruct((B, ATT_HEADS // 2, S, LANES), BF16),
                   jax.ShapeDtypeStruct((B, ATT_HEADS, S // T, VT_ROWS, T), BF16)],
        scratch_shapes=[pltpu.VMEM((LANES, ATT_WIDTH), F32), pltpu.VMEM((ATT_HEADS, 8, LANES), F32)],
        compiler_params=_cparams(2),
        name="moba_prep",
    )(p3, p3, p3, *tables, qg_t, kg)


def _attend(tile_scores, tile_pv, lo, hi, acc_scr, m_scr):
    n_chain = acc_scr.shape[0]

    def run(exact):
        acc_scr[...] = jnp.zeros_like(acc_scr)
        if exact:
            m_scr[...] = jnp.full(m_scr.shape, NEG, F32)

        def tile(j, width, last):
            for c in range(n_chain):
                s_t = tile_scores(c, j, width, last)
                if exact:
                    m_prev = m_scr[c]
                    m_new = jnp.maximum(m_prev, jnp.max(s_t, axis=0, keepdims=True))
                    acc_scr[c] = (acc_scr[c] * jnp.exp2(m_prev - m_new)
                                  + tile_pv(c, j, width, jnp.exp2(s_t - m_new).astype(BF16)))
                    m_scr[c] = m_new
                else:
                    acc_scr[c] += tile_pv(c, j, width, jnp.exp2(s_t).astype(BF16))

        group = 1 if exact else ATTEND_GROUP
        n = hi - lo

        def body(i, carry):
            tile(lo + group * i, group, False)
            return carry

        lax.fori_loop(0, n // group, body, 0)
        start = lo + (n // group) * group
        b = group // 2
        while b:
            take = (n & b) != 0

            @pl.when(take)
            def _(start=start, b=b):
                tile(start, b, False)

            start = start + jnp.where(take, b, 0)
            b //= 2

        tile(hi, 1, True)

    run(False)
    l_min = jnp.min(acc_scr[:, HEAD_DIM:HEAD_DIM + 1, :])

    @pl.when(jnp.logical_not(l_min >= L_TINY))
    def _():
        run(True)


def _head_pair_rows(o_a, o_b):
    return jnp.concatenate([o_a, o_b], axis=0).T


def _normalised_heads(acc_scr, T):
    heads = []
    for c in range(acc_scr.shape[0]):
        o = acc_scr[c, 0:HEAD_DIM, :] / acc_scr[c, HEAD_DIM:HEAD_DIM + 1, :]
        heads += [o[:, h * T:(h + 1) * T] for h in range(o.shape[1] // T)]
    return heads


def _flash_kernel(qt_ref, k_ref, vt_ref, o_ref, acc_scr, m_scr, *, T, mask_block):
    qi = pl.program_id(1)
    P, G, VH = qt_ref.shape[1], k_ref.shape[1], vt_ref.shape[1]
    hc = P // G
    N = hc * T
    q_cat = [jnp.concatenate([qt_ref[0, c * hc + h] for h in range(hc)], axis=1) for c in range(G)]
    key, qry = _iota((T, N), 0), _iota((T, N), 1) % T

    def tile_scores(c, j, width, last):
        rows = width * T
        lane, row = _iota((rows, LANES), 1), _iota((rows, LANES), 0)
        onehot = jnp.where(lane == (j * T + row) // mask_block, 1.0, 0.0).astype(BF16)
        kt = jnp.concatenate([onehot, k_ref[0, c, pl.ds(pl.multiple_of(j * T, T), rows), :]], axis=1)
        s_t = jnp.dot(kt, q_cat[c], preferred_element_type=F32)
        return jnp.where(key <= qry, s_t, NEG) if last else s_t

    def values_t(v, j, width):
        return jnp.concatenate([vt_ref[0, v, j + u] for u in range(width)], axis=1)

    def tile_pv(c, j, width, p):
        if VH == G:
            return jnp.dot(values_t(c, j, width), p, preferred_element_type=F32)
        return jnp.concatenate([jnp.dot(values_t(c * hc + h, j, width), p[:, h * T:(h + 1) * T],
                                        preferred_element_type=F32) for h in range(hc)], axis=1)

    _attend(tile_scores, tile_pv, 0, qi, acc_scr, m_scr)
    heads = _normalised_heads(acc_scr, T)
    for j in range(P // 2):
        o_ref[0, :, j * LANES:(j + 1) * LANES] = _head_pair_rows(heads[2 * j], heads[2 * j + 1]).astype(o_ref.dtype)


def _flash(q_t, k, v_t, out_dtype, mask_block, T=256):
    B, P, _, S = q_t.shape
    T = min(T, S)
    KH, VH = k.shape[1], v_t.shape[1]
    assert v_t.shape == (B, VH, S // T, VT_ROWS, T) and k.shape == (B, KH, S, LANES)
    return pl.pallas_call(
        functools.partial(_flash_kernel, T=T, mask_block=mask_block),
        grid=(B, S // T),
        in_specs=[pl.BlockSpec((1, P, AUG_W, T), lambda b, i: (b, 0, 0, i)),
                  pl.BlockSpec((1, KH, S, LANES), lambda b, i: (b, 0, 0, 0)),
                  pl.BlockSpec((1, VH, S // T, VT_ROWS, T), lambda b, i: (b, 0, 0, 0, 0))],
        out_specs=pl.BlockSpec((1, T, P * HEAD_DIM), lambda b, i: (b, i, 0)),
        out_shape=jax.ShapeDtypeStruct((B, S, P * HEAD_DIM), out_dtype),
        scratch_shapes=[pltpu.VMEM((KH, VT_ROWS, P // KH * T), F32), pltpu.VMEM((KH, 1, P // KH * T), F32)],
        compiler_params=_cparams(2),
        name="flash",
    )(q_t, k, v_t)


def _window_kernel(qt_ref, k_ref, vt_ref, oc_ref, os_ref, sm_ref, y_ref, acc_scr, m_scr, *, T):
    R = ATT_HEADS
    qi = pl.program_id(1)
    q_cat = jnp.concatenate([qt_ref[0, h, LANES:AUG_W, :] for h in range(R)], axis=1)
    def tile_scores(c, j, width, last):
        rows = width * T
        kt = k_ref[0, pl.ds(pl.multiple_of(j * T, T), rows), :]
        s_t = jnp.dot(kt, q_cat, preferred_element_type=F32)
        tq = qi * T + _iota((rows, R * T), 1) % T
        key = j * T + _iota((rows, R * T), 0)
        return jnp.where((key <= tq) & (key > tq - WINDOW), s_t, NEG)

    def tile_pv(c, j, width, p):
        v_t = jnp.concatenate([vt_ref[0, j + u] for u in range(width)], axis=1)
        return jnp.dot(v_t, p, preferred_element_type=F32)

    _attend(tile_scores, tile_pv, jnp.maximum(qi - (WINDOW + T - 1) // T, 0), qi, acc_scr, m_scr)
    sig = jax.nn.sigmoid(sm_ref[0])
    lane = _iota((T, LANES), 1)
    heads = _normalised_heads(acc_scr, T)
    for j in range(R // 2):
        o_w = _head_pair_rows(heads[2 * j], heads[2 * j + 1])

        def gate(branch):
            c = SM_G + branch * R + 2 * j
            return jnp.where(lane < HEAD_DIM, sig[:, c:c + 1], sig[:, c + 1:c + 2])
        sl = slice(j * LANES, (j + 1) * LANES)
        y_ref[0, :, sl] = (gate(0) * oc_ref[0, :, sl] + gate(1) * os_ref[0, :, sl] + gate(2) * o_w).astype(y_ref.dtype)


def _window_combine(q_t, kw, vw_t, o_c, o_s, p3, T=256):
    B, R, _, S = q_t.shape
    T = min(T, S)
    packed = pl.BlockSpec((1, T, ATT_WIDTH), lambda b, i: (b, i, 0))
    return pl.pallas_call(
        functools.partial(_window_kernel, T=T),
        grid=(B, S // T),
        in_specs=[pl.BlockSpec((1, R, AUG_W, T), lambda b, i: (b, 0, 0, i)),
                  pl.BlockSpec((1, S, LANES), lambda b, i: (b, 0, 0)),
                  pl.BlockSpec((1, S // T, VT_ROWS, T), lambda b, i: (b, 0, 0, 0)),
                  packed, packed,
                  pl.BlockSpec((1, T, LANES), lambda b, i: (b, i, COL_SM // LANES))],
        out_specs=packed,
        out_shape=jax.ShapeDtypeStruct((B, S, ATT_WIDTH), BF16),
        scratch_shapes=[pltpu.VMEM((1, VT_ROWS, R * T), F32), pltpu.VMEM((1, 1, R * T), F32)],
        compiler_params=_cparams(2),
        name="nsa_window",
    )(q_t, kw, vw_t, o_c, o_s, p3)


def _compress_kernel(t_ref, pea_ref, peb_ref, w1a_ref, w1b_ref, w2_ref, kg_ref, kc_ref, vct_ref, t_scr):
    n = t_scr.shape[0]
    for r in range(CMP_STRIDE):
        t_scr[:, r * LANES:(r + 1) * LANES] = t_ref[0, pl.ds(r, n, stride=CMP_STRIDE), :]
    t = t_scr[...]
    a = jnp.dot((t + pea_ref[...]).astype(BF16), w1a_ref[...], preferred_element_type=F32)
    b = jnp.dot((t + peb_ref[...]).astype(BF16), w1b_ref[...], preferred_element_type=F32)
    hid = a + pltpu.roll(b, n - 1, 0)
    hid = hid * jax.nn.sigmoid(hid)
    kv = jnp.dot(hid.astype(BF16), w2_ref[...], preferred_element_type=F32)
    lane = _iota(kv.shape, 1)
    ms = jnp.sum(jnp.where(lane < HEAD_DIM, kv * kv, 0.0), axis=-1, keepdims=True) * (1.0 / HEAD_DIM)
    kc_ref[0] = jnp.where(lane < HEAD_DIM, kv * lax.rsqrt(ms + NORM_EPS) * kg_ref[...], 0.0).astype(kc_ref.dtype)
    vct_ref[0] = kv.T[HEAD_DIM:, :].astype(vct_ref.dtype)


def _compress(p3, pea, peb, w1a, w1b, w2, kg):
    B, S, _ = p3.shape
    n, W = S // CMP_STRIDE, CMP_STRIDE * LANES
    return pl.pallas_call(
        _compress_kernel,
        grid=(B,),
        in_specs=[pl.BlockSpec((1, S, LANES), lambda b: (b, 0, COL_NKV // LANES)),
                  _const_spec((1, W)), _const_spec((1, W)),
                  _const_spec(w1a.shape), _const_spec(w1b.shape), _const_spec(w2.shape),
                  _const_spec((1, LANES))],
        out_specs=[pl.BlockSpec((1, n, LANES), lambda b: (b, 0, 0)),
                   pl.BlockSpec((1, HEAD_DIM, n), lambda b: (b, 0, 0))],
        out_shape=[jax.ShapeDtypeStruct((B, n, LANES), BF16), jax.ShapeDtypeStruct((B, HEAD_DIM, n), BF16)],
        scratch_shapes=[pltpu.VMEM((n, W), F32)],
        compiler_params=_cparams(1),
        name="nsa_compress",
    )(p3, pea, peb, w1a, w1b, w2, kg)


def _nsa_prep_kernel(q_ref, kv_ref, cos_ref, sin_ref, cost_ref, sint_ref, qg_ref, kg_ref, kc_ref, vct_ref, ovt_ref,
                     oc_ref, qt_ref, ks_ref, kw_ref, vst_ref, vwt_ref, kmax_scr, *, n_sel):
    i = pl.program_id(1)
    T = q_ref.shape[1]
    nc = kc_ref.shape[1]

    @pl.when(i == 0)
    def _():
        kmax_scr[...] = jnp.zeros_like(kmax_scr)

    cos, sin = cos_ref[...], sin_ref[...]
    lane = _iota((T, LANES), 1)
    row = _iota((LANES, T), 0)
    tq = i * T + _iota((1, T), 1)

    qn_t = _rms_row_groups(q_ref[0].T, HEAD_DIM) * qg_ref[...]
    qr_t = _rope_t(qn_t, cost_ref[...], sint_ref[...])
    zeros_half = jnp.zeros((HEAD_DIM, T), F32)

    kc, vc_t = kc_ref[0], vct_ref[0]
    valid = _iota((nc, T), 0) * CMP_STRIDE + (CMP_LEN - 1) <= tq
    psum = jnp.zeros((nc, T), F32)
    o_heads = []
    for h in range(ATT_HEADS):
        qh = jnp.concatenate([qn_t[h * HEAD_DIM:(h + 1) * HEAD_DIM] * SCALE, zeros_half], axis=0).astype(BF16)
        s = jnp.where(valid, jnp.dot(kc, qh, preferred_element_type=F32), NEG)
        e = jnp.exp(s - jnp.max(s, axis=0, keepdims=True))
        p = jnp.where(valid, e / jnp.sum(e, axis=0, keepdims=True), 0.0)
        o_heads.append(jnp.dot(vc_t, p.astype(BF16), preferred_element_type=F32))
        psum = psum + p
    for j in range(ATT_HEADS // 2):
        oc_ref[0, :, j * LANES:(j + 1) * LANES] = _head_pair_rows(o_heads[2 * j], o_heads[2 * j + 1])

    imp = jnp.dot(ovt_ref[...], psum, precision=_HI, preferred_element_type=F32)
    blk_q = tq // SEL_BLOCK
    causal_blk = row <= blk_q
    forced = causal_blk & ((row == 0) | (row >= blk_q - 1))
    imp = jnp.where(forced, BIG, jnp.where(causal_blk, imp, NEG))
    imp = jnp.where(row < n_sel, imp, -jnp.inf)
    sel = _top_k_mask_t(imp, min(SEL_TOPK, n_sel))
    bias = jnp.where(sel > 0, 0.0, NEG)

    cos_k = jnp.where(lane < HEAD_DIM, cos[:, :LANES], 1.0)
    sin_k = jnp.where(lane < HEAD_DIM, sin[:, :LANES], 0.0)

    def key_pair(x, gain, idx):
        ms = jnp.sum(jnp.where(lane < HEAD_DIM, x * x, 0.0), axis=-1, keepdims=True) * (1.0 / HEAD_DIM)
        kn = _rope(x * lax.rsqrt(ms + NORM_EPS) * gain, cos_k, sin_k)
        kb = jnp.where(lane < HEAD_DIM, kn, 0.0).astype(BF16).astype(F32)
        return kb, _value_tile_t(x.T[HEAD_DIM:, :]), _running_max(kmax_scr, idx, _group_norms(kb, LANES)[0])

    ks, vst_ref[0, 0], ks_max = key_pair(kv_ref[0, :, LANES:2 * LANES], kg_ref[0:1, :], 0)
    kw, vwt_ref[0, 0], kw_max = key_pair(kv_ref[0, :, 2 * LANES:3 * LANES], kg_ref[1:2, :], 1)
    ks_ref[0] = ks.astype(BF16)
    kw_ref[0] = jnp.where(lane == HEAD_DIM, 1.0, kw).astype(BF16)

    q2_t = (qr_t * SCALE_LOG2).astype(BF16).astype(F32)
    row_h = _iota((HEAD_DIM, T), 0)
    for h, q_norm in enumerate(_row_group_norms(q2_t, HEAD_DIM)):
        qt_ref[0, h, 0:LANES, :] = (bias - q_norm * (ks_max * BOUND_SLACK)).astype(BF16)
        shift_rows = jnp.where(row_h == 0, -q_norm * (kw_max * BOUND_SLACK), 0.0)
        qt_ref[0, h, LANES:AUG_W, :] = jnp.concatenate([q2_t[h * HEAD_DIM:(h + 1) * HEAD_DIM], shift_rows],
                                                       axis=0).astype(BF16)


def _nsa_prep(p3, tables, qg_t, kg2, kc, vc_t, overlap_t, T=256):
    B, S, _ = p3.shape
    T = min(T, S)
    nc = kc.shape[1]
    n_sel = S // SEL_BLOCK
    tab = pl.BlockSpec((T, ATT_WIDTH), lambda b, i: (i, 0))
    tab_t = pl.BlockSpec((ATT_WIDTH, T), lambda b, i: (0, i))
    v_t = pl.BlockSpec((1, 1, VT_ROWS, T), lambda b, i: (b, i, 0, 0))
    v_t_shape = jax.ShapeDtypeStruct((B, S // T, VT_ROWS, T), BF16)
    return pl.pallas_call(
        functools.partial(_nsa_prep_kernel, n_sel=n_sel),
        grid=(B, S // T),
        in_specs=[pl.BlockSpec((1, T, ATT_WIDTH), lambda b, i: (b, i, COL_NQ // ATT_WIDTH)),
                  pl.BlockSpec((1, T, 3 * LANES), lambda b, i: (b, i, COL_NKV // (3 * LANES))),
                  tab, tab, tab_t, tab_t,
                  _const_spec((ATT_WIDTH, T)), _const_spec((2, LANES)),
                  pl.BlockSpec((1, nc, LANES), lambda b, i: (b, 0, 0)),
                  pl.BlockSpec((1, HEAD_DIM, nc), lambda b, i: (b, 0, 0)),
                  _const_spec((LANES, nc))],
        out_specs=[pl.BlockSpec((1, T, ATT_WIDTH), lambda b, i: (b, i, 0)),
                   pl.BlockSpec((1, ATT_HEADS, AUG_W, T), lambda b, i: (b, 0, 0, i)),
                   pl.BlockSpec((1, T, LANES), lambda b, i: (b, i, 0)),
                   pl.BlockSpec((1, T, LANES), lambda b, i: (b, i, 0)),
                   v_t, v_t],
        out_shape=[jax.ShapeDtypeStruct((B, S, ATT_WIDTH), F32),
                   jax.ShapeDtypeStruct((B, ATT_HEADS, AUG_W, S), BF16),
                   jax.ShapeDtypeStruct((B, S, LANES), BF16),
                   jax.ShapeDtypeStruct((B, S, LANES), BF16),
                   v_t_shape, v_t_shape],
        scratch_shapes=[pltpu.VMEM((2, 8, LANES), F32)],
        compiler_params=_cparams(2),
        name="nsa_prep",
    )(p3, p3, *tables, qg_t, kg2, kc, vc_t, overlap_t)


def _out_ffn_kernel(x_ref, ym_ref, yb_ref, yn_ref, wo_ref, g_ref, w1_ref, w2_ref, o_ref, *, fc):
    mix = jnp.concatenate([ym_ref[...], yb_ref[...], yn_ref[...]], axis=1)
    x = x_ref[...] + jnp.dot(mix, wo_ref[...], preferred_element_type=F32)
    hb = (x * lax.rsqrt(jnp.mean(x * x, axis=-1, keepdims=True) + NORM_EPS) * g_ref[...]).astype(BF16)
    acc = x
    for c in range(w1_ref.shape[1] // fc):
        u = jnp.maximum(jnp.dot(hb, w1_ref[:, c * fc:(c + 1) * fc], preferred_element_type=F32), 0.0)
        acc = acc + jnp.dot((u * u).astype(BF16), w2_ref[c * fc:(c + 1) * fc, :], preferred_element_type=F32)
    o_ref[...] = acc


def _out_ffn(x2d, y_m, y_b, y_n, wo, g, w1, w2, tm=512, fc=1024):
    M, D = x2d.shape
    tm = min(tm, M)
    rows = lambda w: pl.BlockSpec((tm, w), lambda i: (i, 0))
    resident = lambda a: pl.BlockSpec(a.shape, lambda i: (0,) * a.ndim, pipeline_mode=pl.Buffered(1))
    return pl.pallas_call(
        functools.partial(_out_ffn_kernel, fc=fc),
        grid=(M // tm,),
        in_specs=[rows(D), rows(M_WIDTH), rows(ATT_WIDTH), rows(ATT_WIDTH),
                  resident(wo), _const_spec((1, D)), resident(w1), resident(w2)],
        out_specs=rows(D),
        out_shape=jax.ShapeDtypeStruct((M, D), F32),
        compiler_params=_cparams(1),
        name="out_ffn",
    )(x2d, y_m, y_b, y_n, wo, g, w1, w2)


def _rope_tables(S):
    inv_freq = jnp.exp(-math.log(ROPE_THETA) * jnp.arange(ROT_HALF, dtype=F32) * (2.0 / ROT_DIM))
    ang = jnp.arange(S, dtype=F32)[:, None] * inv_freq[None, :]
    cos, sin = jnp.cos(ang), jnp.sin(ang)
    rest = HEAD_DIM - ROT_DIM
    cos64 = jnp.concatenate([cos, cos, jnp.ones((S, rest), F32)], axis=1)
    sin64 = jnp.concatenate([-sin, sin, jnp.zeros((S, rest), F32)], axis=1)
    return jnp.tile(cos64, (1, ATT_HEADS)), jnp.tile(sin64, (1, ATT_HEADS))


def _overlap_matrix(nc):
    c_start = np.arange(nc)[:, None] * CMP_STRIDE
    s_start = np.arange(LANES)[None, :] * SEL_BLOCK
    return jnp.asarray(((c_start < s_start + SEL_BLOCK) & (c_start + CMP_LEN > s_start)).astype(np.float32))


def _compress_weights(pe, w1, w2):
    half = CMP_LEN // 2
    pe_r = jnp.concatenate([pe[0], pe[1]], axis=-1)
    pea = pe_r[:half].reshape(1, half * LANES)
    peb = pe_r[half:].reshape(1, half * LANES)
    w1r = w1.reshape(2, CMP_LEN, HEAD_DIM, CMP_HIDDEN)
    z = jnp.zeros_like(w1r[0])
    wk = jnp.concatenate([w1r[0], z], axis=-1)
    wv = jnp.concatenate([z, w1r[1]], axis=-1)
    wboth = jnp.concatenate([wk, wv], axis=1)
    w1a = wboth[:half].reshape(half * LANES, 2 * CMP_HIDDEN).astype(BF16)
    w1b = wboth[half:].reshape(half * LANES, 2 * CMP_HIDDEN).astype(BF16)
    z2 = jnp.zeros_like(w2[0])
    w2bd = jnp.concatenate([jnp.concatenate([w2[0], z2], axis=1),
                            jnp.concatenate([z2, w2[1]], axis=1)], axis=0).astype(BF16)
    return pea, peb, w1a, w1b, w2bd


def _pad_lanes(v, width=LANES):
    return jnp.concatenate([v, jnp.zeros((width - v.shape[0],), v.dtype)])[None, :]


def kernel(x, w_in, b_if, conv_qk, m_norm, moba_qk_norm, nsa_q_norm, nsa_k_norm, cmp_pe, cmp_w1, cmp_w2,
           w_out, norm_mix, norm_ffn, w_ff1, w_ff2):
    B, S, D = x.shape
    depth = w_in.shape[0]
    cos, sin = _rope_tables(S)
    tables = (cos, sin, cos.T, sin.T)
    overlap_t = _overlap_matrix(S // CMP_STRIDE).T
    x2d = x.reshape(B * S, D)
    for l in range(depth):
        p3 = _proj(x2d, norm_mix[l][None, :], w_in, l).reshape(B, S, P_W)

        y_m = _mlstm(p3, conv_qk[l], _pad_lanes(b_if[l]), m_norm[l][None, :])

        tile_g = lambda g: jnp.tile(g, ATT_HEADS)[None, :]
        rows_g = lambda g: jnp.broadcast_to(jnp.tile(g, ATT_HEADS)[:, None], (ATT_WIDTH, min(MOBA_BLOCK, S)))
        qt_b, kp_b, vt_b = _moba_prep(p3, tables, rows_g(moba_qk_norm[l, 0]), tile_g(moba_qk_norm[l, 1]))
        y_b = _flash(qt_b, kp_b, vt_b, BF16, MOBA_BLOCK)

        pea, peb, w1a, w1b, w2bd = _compress_weights(cmp_pe[l], cmp_w1[l], cmp_w2[l])
        kc, vc_t = _compress(p3, pea, peb, w1a, w1b, w2bd, _pad_lanes(nsa_k_norm[l, 0]))
        kg2 = jnp.concatenate([_pad_lanes(nsa_k_norm[l, 1]), _pad_lanes(nsa_k_norm[l, 2])], axis=0)
        o_c, qt_n, ks, kw, vs_t, vw_t = _nsa_prep(p3, tables, rows_g(nsa_q_norm[l]), kg2, kc, vc_t, overlap_t)
        o_s = _flash(qt_n, ks[:, None], vs_t[:, None], F32, SEL_BLOCK)
        y_n = _window_combine(qt_n, kw, vw_t, o_c, o_s, p3)

        x2d = _out_ffn(x2d, y_m.reshape(B * S, M_WIDTH), y_b.reshape(B * S, ATT_WIDTH), y_n.reshape(B * S, ATT_WIDTH),
                       w_out[l].astype(BF16), norm_ffn[l][None, :], w_ff1[l].astype(BF16), w_ff2[l].astype(BF16))
    return x2d.reshape(B, S, D)
```

```python
import functools
import math

import jax
import jax.numpy as jnp
import numpy as np
from jax import lax
from jax.experimental import pallas as pl
from jax.experimental.pallas import tpu as pltpu

F32 = jnp.float32
BF16 = jnp.bfloat16

HEAD_DIM = 64
M_HEADS = 4
M_HEAD_DIM = 128
M_WIDTH = M_HEADS * M_HEAD_DIM
CONV_W = 4
ATT_HEADS = 4
ATT_WIDTH = ATT_HEADS * HEAD_DIM
MOBA_BLOCK = 256
MOBA_TOPK = 3
CMP_LEN = 32
CMP_STRIDE = 16
CMP_HIDDEN = 128
SEL_BLOCK = 64
SEL_TOPK = 16
WINDOW = 512
ROPE_THETA = 500000.0
ROT_DIM = HEAD_DIM // 4
ROT_HALF = ROT_DIM // 2
NORM_EPS = 1e-6
NEG = -1e30
BIG = 1e9
SCALE = HEAD_DIM ** -0.5
SCALE_LOG2 = SCALE * math.log2(math.e)
BOUND_SLACK = 1.0 + 2.0 ** -7
L_TINY = 2.0 ** -100

LANES = 128
AUG_W = 2 * LANES
VT_ROWS = HEAD_DIM + 16
ATTEND_GROUP = 4
VMEM_LIMIT = 56 * 1024 * 1024

COL_MQK, COL_MV, COL_MO = 0, 1024, 1536
COL_BQ, COL_BK, COL_BV = 2048, 2304, 2560
COL_NQ, COL_NKV, COL_SM = 2816, 3072, 3456
P_W = 3584
SM_I, SM_F, SM_G = 0, 4, 8
_W_ATT0 = 4 * M_WIDTH + 2 * M_HEADS
_W_ATT1 = _W_ATT0 + 4 * ATT_WIDTH + 6 * HEAD_DIM
W_IN_SECTIONS = ((0, 4 * M_WIDTH), (_W_ATT0, _W_ATT1 - _W_ATT0), (4 * M_WIDTH, 2 * M_HEADS), (_W_ATT1, 3 * ATT_HEADS))

_NT = (((1,), (1,)), ((), ()))
_HI = lax.Precision.HIGHEST


def _iota(shape, dim):
    return lax.broadcasted_iota(jnp.int32, shape, dim)


def _cparams(n_axes):
    return pltpu.CompilerParams(dimension_semantics=("arbitrary",) * n_axes,
                                vmem_limit_bytes=VMEM_LIMIT)


def _const_spec(shape):
    nd = len(shape)
    return pl.BlockSpec(shape, lambda *_: (0,) * nd)


def _rms_groups(x, width):
    T, W = x.shape
    x2 = x * x
    lane = _iota((T, W), 1)
    scale = None
    for h in range(W // width):
        r = lax.rsqrt(jnp.mean(x2[:, h * width:(h + 1) * width], axis=-1, keepdims=True) + NORM_EPS)
        scale = r if scale is None else jnp.where(lane >= h * width, r, scale)
    return x * scale


def _rope(x, cos, sin):
    W = x.shape[1]
    lane = _iota(x.shape, 1) % HEAD_DIM
    up = pltpu.roll(x, W - ROT_HALF, 1)
    dn = pltpu.roll(x, ROT_HALF, 1)
    return x * cos + jnp.where(lane < ROT_HALF, up, dn) * sin


def _head_piece(x, h):
    pair = x[:, (h // 2) * LANES:(h // 2 + 1) * LANES]
    if h % 2:
        pair = pltpu.roll(pair, HEAD_DIM, 1)
    return jnp.where(_iota(pair.shape, 1) < HEAD_DIM, pair, 0.0)


def _group_norms(x, width):
    x2 = x * x
    return [jnp.sqrt(jnp.sum(x2[:, h * width:(h + 1) * width], axis=-1, keepdims=True))
            for h in range(x.shape[1] // width)]


def _running_max(scr, idx, col):
    new = jnp.maximum(scr[idx][0:1, 0:1], jnp.max(col, axis=0, keepdims=True))
    scr[idx] = jnp.broadcast_to(new, scr.shape[1:])
    return new


def _value_tile_t(v_t):
    extra = jnp.where(_iota((VT_ROWS - HEAD_DIM, v_t.shape[1]), 0) == 0, 1.0, 0.0)
    return jnp.concatenate([v_t, extra], axis=0).astype(BF16)


def _rms_row_groups(x_t, width):
    parts = []
    for h in range(x_t.shape[0] // width):
        g = x_t[h * width:(h + 1) * width]
        parts.append(g * lax.rsqrt(jnp.mean(g * g, axis=0, keepdims=True) + NORM_EPS))
    return jnp.concatenate(parts, axis=0)


def _rope_t(x_t, cos_t, sin_t):
    W = x_t.shape[0]
    row = _iota(x_t.shape, 0) % HEAD_DIM
    up = pltpu.roll(x_t, W - ROT_HALF, 0)
    dn = pltpu.roll(x_t, ROT_HALF, 0)
    return x_t * cos_t + jnp.where(row < ROT_HALF, up, dn) * sin_t


def _row_group_norms(x_t, width):
    return [jnp.sqrt(jnp.sum(jnp.square(x_t[h * width:(h + 1) * width]), axis=0, keepdims=True))
            for h in range(x_t.shape[0] // width)]


def _top_k_mask_t(vals, k, rank_limit=None):
    row = _iota(vals.shape, 0)
    sel = jnp.zeros(vals.shape, jnp.int32)
    for r in range(k):
        mx = jnp.max(vals, axis=0, keepdims=True)
        idx = jnp.min(jnp.where(vals == mx, row, LANES), axis=0, keepdims=True)
        pick = row == idx
        mark = 1 if rank_limit is None else jnp.where(rank_limit > r, 1, 0)
        sel = jnp.where(pick, mark, sel)
        vals = jnp.where(pick, -jnp.inf, vals)
    return sel


def _proj_kernel(x_ref, g_ref, w_ref, o_ref, w_scr):
    @pl.when(pl.program_id(0) == 0)
    def _():
        dst = 0
        for src, width in W_IN_SECTIONS:
            w_scr[:, dst:dst + width] = w_ref[0, :, src:src + width].astype(BF16)
            dst += width
        w_scr[:, dst:] = jnp.zeros((w_scr.shape[0], w_scr.shape[1] - dst), BF16)

    x = x_ref[...]
    h = x * lax.rsqrt(jnp.mean(x * x, axis=-1, keepdims=True) + NORM_EPS) * g_ref[...]
    o_ref[...] = jnp.dot(h.astype(BF16), w_scr[...], preferred_element_type=F32)


def _proj(x2d, g, w_all, layer, tm=512):
    M, D = x2d.shape
    N = P_W
    assert w_all.shape[2] == sum(width for _, width in W_IN_SECTIONS)
    return pl.pallas_call(
        _proj_kernel,
        grid=(M // tm,),
        in_specs=[pl.BlockSpec((tm, D), lambda i: (i, 0)), _const_spec((1, D)),
                  pl.BlockSpec((1, D, w_all.shape[2]), lambda i: (layer, 0, 0), pipeline_mode=pl.Buffered(1))],
        out_specs=pl.BlockSpec((tm, N), lambda i: (i, 0)),
        out_shape=jax.ShapeDtypeStruct((M, N), F32),
        scratch_shapes=[pltpu.VMEM((D, N), BF16)],
        compiler_params=_cparams(1),
        name="in_proj",
    )(x2d, g, w_all)


def _mlstm_kernel(qk_ref, v_ref, o_ref, sm_ref, cw_ref, b_ref, g_ref, out_ref, xbuf, c_scr, m_scr, *, Lc):
    W2 = 2 * M_WIDTH
    D = M_HEAD_DIM

    @pl.when(pl.program_id(1) == 0)
    def _():
        xbuf[0:8, :] = jnp.zeros((8, W2), F32)
        c_scr[...] = jnp.zeros_like(c_scr)
        m_scr[...] = jnp.zeros_like(m_scr)

    x = qk_ref[0]
    xbuf[8:8 + Lc, :] = x
    y = cw_ref[0:1, :] * xbuf[8 - CONV_W + 1:8 - CONV_W + 1 + Lc, :]
    for j in range(1, CONV_W):
        y = y + cw_ref[j:j + 1, :] * xbuf[8 - CONV_W + 1 + j:8 - CONV_W + 1 + j + Lc, :]
    xbuf[0:8, :] = x[Lc - 8:Lc, :]
    qk = y * jax.nn.sigmoid(y)

    gi = sm_ref[0] + b_ref[...]
    lane = _iota((Lc, LANES), 1)
    lsig = jnp.minimum(gi, 0.0) - jnp.log1p(jnp.exp(-jnp.abs(gi)))
    gates = jnp.where(lane < SM_F, gi, jnp.where(lane < SM_F + M_HEADS, lsig, 0.0))
    gates_t = gates.T
    ri, ci = _iota((Lc, Lc), 0), _iota((Lc, Lc), 1)
    causal = ci <= ri
    b_col = jnp.dot(causal.astype(F32), gates, precision=_HI, preferred_element_type=F32)
    b_row = jnp.dot(gates_t[0:8, :], (ri <= ci).astype(F32), precision=_HI, preferred_element_type=F32)
    ones_col = jnp.where(lane == 0, 1.0, 0.0)

    for h in range(M_HEADS):
        q = qk[:, h * D:(h + 1) * D]
        k = qk[:, M_WIDTH + h * D:M_WIDTH + (h + 1) * D] * (D ** -0.5)
        v_aug = jnp.concatenate([v_ref[0, :, h * D:(h + 1) * D], ones_col], axis=1).astype(BF16)
        bc = b_col[:, SM_F + h:SM_F + h + 1]
        br = b_row[SM_F + h:SM_F + h + 1, :]
        li_r = gates_t[SM_I + h:SM_I + h + 1, :]
        li_c = gates[:, SM_I + h:SM_I + h + 1]
        m_prev = m_scr[h][0:1, 0:1]

        dmat = jnp.where(causal, bc - br + li_r, -jnp.inf)
        inter = bc + m_prev
        m_t = jnp.maximum(inter, jnp.max(dmat, axis=-1, keepdims=True))
        w_intra = jnp.exp(dmat - m_t)
        w_prev = jnp.exp(inter - m_t)
        qb = q.astype(BF16)
        s = lax.dot_general(qb, k.astype(BF16), _NT, preferred_element_type=F32) * w_intra
        tot = (jnp.dot(s.astype(BF16), v_aug, preferred_element_type=F32)
               + w_prev * jnp.dot(qb, c_scr[h].astype(BF16), preferred_element_type=F32))
        den = tot[:, D:D + 1]
        hh = tot[:, :D] / jnp.maximum(jnp.abs(den), jnp.exp(-m_t))

        b_last = bc[Lc - 1:Lc, :]
        g_c = b_last - bc + li_c
        m_new = jnp.maximum(b_last + m_prev, jnp.max(g_c, axis=0, keepdims=True))
        a = jnp.exp(b_last + m_prev - m_new)
        kw_t = (k * jnp.exp(g_c - m_new)).T.astype(BF16)
        c_scr[h] = a * c_scr[h] + jnp.dot(kw_t, v_aug, preferred_element_type=F32)
        m_scr[h] = jnp.broadcast_to(m_new, (8, LANES))

        hn = hh * lax.rsqrt(jnp.mean(hh * hh, axis=-1, keepdims=True) + NORM_EPS) * g_ref[0:1, h * D:(h + 1) * D]
        out_ref[0, :, h * D:(h + 1) * D] = (jax.nn.sigmoid(o_ref[0, :, h * D:(h + 1) * D]) * hn).astype(BF16)


def _mlstm(p3, conv_w, b_sm, m_norm, Lc=256):
    B, S, _ = p3.shape
    Lc = min(Lc, S)
    kern = functools.partial(_mlstm_kernel, Lc=Lc)
    return pl.pallas_call(
        kern,
        grid=(B, S // Lc),
        in_specs=[
            pl.BlockSpec((1, Lc, 2 * M_WIDTH), lambda b, c: (b, c, COL_MQK // (2 * M_WIDTH))),
            pl.BlockSpec((1, Lc, M_WIDTH), lambda b, c: (b, c, COL_MV // M_WIDTH)),
            pl.BlockSpec((1, Lc, M_WIDTH), lambda b, c: (b, c, COL_MO // M_WIDTH)),
            pl.BlockSpec((1, Lc, LANES), lambda b, c: (b, c, COL_SM // LANES)),
            _const_spec((CONV_W, 2 * M_WIDTH)),
            _const_spec((1, LANES)),
            _const_spec((1, M_WIDTH)),
        ],
        out_specs=pl.BlockSpec((1, Lc, M_WIDTH), lambda b, c: (b, c, 0)),
        out_shape=jax.ShapeDtypeStruct((B, S, M_WIDTH), BF16),
        scratch_shapes=[
            pltpu.VMEM((Lc + 8, 2 * M_WIDTH), F32),
            pltpu.VMEM((M_HEADS, M_HEAD_DIM, 2 * M_HEAD_DIM), F32),
            pltpu.VMEM((M_HEADS, 8, LANES), F32),
        ],
        compiler_params=_cparams(2),
        name="mlstm",
    )(p3, p3, p3, p3, conv_w, b_sm, m_norm)


def _moba_prep_kernel(q_ref, k_ref, v_ref, cos_ref, sin_ref, cost_ref, sint_ref, qg_ref, kg_ref,
                      qt_ref, kp_ref, vt_ref, kmean_scr, kmax_scr):
    i = pl.program_id(1)
    T = q_ref.shape[1]

    @pl.when(i == 0)
    def _():
        kmean_scr[...] = jnp.zeros_like(kmean_scr)
        kmax_scr[...] = jnp.zeros_like(kmax_scr)

    kn = _rope(_rms_groups(k_ref[0], HEAD_DIM) * kg_ref[...], cos_ref[...], sin_ref[...])
    kb = kn.astype(BF16).astype(F32)
    k_norms = _group_norms(kb, HEAD_DIM)
    qn_t = _rope_t(_rms_row_groups(q_ref[0].T, HEAD_DIM) * qg_ref[...], cost_ref[...], sint_ref[...])
    q2_t = (qn_t * SCALE_LOG2).astype(BF16).astype(F32)
    q_norms = _row_group_norms(q2_t, HEAD_DIM)
    v_t = v_ref[0].T
    row_w = _iota((ATT_WIDTH, T), 0)
    row = _iota((LANES, T), 0)
    zeros_half = jnp.zeros((HEAD_DIM, T), F32)
    kmeans = kmean_scr[...]
    for h in range(ATT_HEADS):
        qm = jnp.where((row_w >= h * HEAD_DIM) & (row_w < (h + 1) * HEAD_DIM), qn_t, 0.0)
        gs = jnp.dot(kmeans, qm, precision=_HI, preferred_element_type=F32)
        sel = _top_k_mask_t(jnp.where(row < i, gs, NEG), MOBA_TOPK, rank_limit=i)
        bound = q_norms[h] * (_running_max(kmax_scr, h, k_norms[h]) * BOUND_SLACK)
        qt_ref[0, h, 0:LANES, :] = (jnp.where((sel > 0) | (row == i), 0.0, NEG) - bound).astype(BF16)
        q_h = q2_t[h * HEAD_DIM:(h + 1) * HEAD_DIM]
        qt_ref[0, h, LANES:AUG_W, :] = jnp.concatenate([zeros_half, q_h] if h % 2 else [q_h, zeros_half],
                                                       axis=0).astype(BF16)
        vt_ref[0, h, 0] = _value_tile_t(v_t[h * HEAD_DIM:(h + 1) * HEAD_DIM, :])
    for j in range(ATT_HEADS // 2):
        kp_ref[0, j] = kb[:, j * LANES:(j + 1) * LANES].astype(BF16)
    kmean_scr[pl.ds(i, 1), :] = jnp.mean(kn, axis=0, keepdims=True)


def _moba_prep(p3, tables, qg_t, kg):
    B, S, _ = p3.shape
    T = MOBA_BLOCK
    blk = lambda col: pl.BlockSpec((1, T, ATT_WIDTH), lambda b, i: (b, i, col // ATT_WIDTH))
    tab = pl.BlockSpec((T, ATT_WIDTH), lambda b, i: (i, 0))
    tab_t = pl.BlockSpec((ATT_WIDTH, T), lambda b, i: (0, i))
    return pl.pallas_call(
        _moba_prep_kernel,
        grid=(B, S // T),
        in_specs=[blk(COL_BQ), blk(COL_BK), blk(COL_BV), tab, tab, tab_t, tab_t,
                  _const_spec((ATT_WIDTH, T)), _const_spec((1, ATT_WIDTH))],
        out_specs=[pl.BlockSpec((1, ATT_HEADS, AUG_W, T), lambda b, i: (b, 0, 0, i)),
                   pl.BlockSpec((1, ATT_HEADS // 2, T, LANES), lambda b, i: (b, 0, i, 0)),
                   pl.BlockSpec((1, ATT_HEADS, 1, VT_ROWS, T), lambda b, i: (b, 0, i, 0, 0))],
        out_shape=[jax.ShapeDtypeStruct((B, ATT_HEADS, AUG_W, S), BF16),
                   jax.ShapeDtypeStruct((B, ATT_HEADS // 2, S, LANES), BF16),
                   jax.ShapeDtypeStruct((B, ATT_HEADS, S // T, VT_ROWS, T), BF16)],
        scratch_shapes=[pltpu.VMEM((LANES, ATT_WIDTH), F32), pltpu.VMEM((ATT_HEADS, 8, LANES), F32)],
        compiler_params=_cparams(2),
        name="moba_prep",
    )(p3, p3, p3, *tables, qg_t, kg)


def _attend(tile_scores, tile_pv, lo, hi, acc_scr, m_scr, own_width=1):
    n_chain = acc_scr.shape[0]

    def run(exact):
        acc_scr[...] = jnp.zeros_like(acc_scr)
        if exact:
            m_scr[...] = jnp.full(m_scr.shape, NEG, F32)

        def tile(j, width, last):
            for c in range(n_chain):
                s_t = tile_scores(c, j, width, last)
                if exact:
                    m_prev = m_scr[c]
                    m_new = jnp.maximum(m_prev, jnp.max(s_t, axis=0, keepdims=True))
                    acc_scr[c] = (acc_scr[c] * jnp.exp2(m_prev - m_new)
                                  + tile_pv(c, j, width, jnp.exp2(s_t - m_new).astype(BF16)))
                    m_scr[c] = m_new
                else:
                    acc_scr[c] += tile_pv(c, j, width, jnp.exp2(s_t).astype(BF16))

        group = 1 if exact else ATTEND_GROUP
        n = hi - lo

        def body(i, carry):
            tile(lo + group * i, group, False)
            return carry

        lax.fori_loop(0, n // group, body, 0)
        start = lo + (n // group) * group
        b = group // 2
        while b:
            take = (n & b) != 0

            @pl.when(take)
            def _(start=start, b=b):
                tile(start, b, False)

            start = start + jnp.where(take, b, 0)
            b //= 2

        tile(hi, own_width, True)

    run(False)
    l_min = jnp.min(acc_scr[:, HEAD_DIM:HEAD_DIM + 1, :])

    @pl.when(jnp.logical_not(l_min >= L_TINY))
    def _():
        run(True)


def _head_pair_rows(o_a, o_b):
    return jnp.concatenate([o_a, o_b], axis=0).T


def _normalised_heads(acc_scr, T):
    heads = []
    for c in range(acc_scr.shape[0]):
        o = acc_scr[c, 0:HEAD_DIM, :] / acc_scr[c, HEAD_DIM:HEAD_DIM + 1, :]
        heads += [o[:, h * T:(h + 1) * T] for h in range(o.shape[1] // T)]
    return heads


def _flash_kernel(qt_ref, k_ref, vt_ref, o_ref, acc_scr, m_scr, *, TQ, mask_block):
    qi = pl.program_id(1)
    P, G, VH = qt_ref.shape[1], k_ref.shape[1], vt_ref.shape[1]
    TK = vt_ref.shape[4]
    own = TQ // TK
    hc = P // G
    N = hc * TQ
    q_cat = [jnp.concatenate([qt_ref[0, c * hc + h] for h in range(hc)], axis=1) for c in range(G)]
    key, qry = _iota((TQ, N), 0), _iota((TQ, N), 1) % TQ

    def tile_scores(c, j, width, last):
        rows = width * TK
        lane, row = _iota((rows, LANES), 1), _iota((rows, LANES), 0)
        onehot = jnp.where(lane == (j * TK + row) // mask_block, 1.0, 0.0).astype(BF16)
        kt = jnp.concatenate([onehot, k_ref[0, c, pl.ds(pl.multiple_of(j * TK, TK), rows), :]], axis=1)
        s_t = jnp.dot(kt, q_cat[c], preferred_element_type=F32)
        return jnp.where(key <= qry, s_t, NEG) if last else s_t

    def values_t(v, j, width):
        return jnp.concatenate([vt_ref[0, v, j + u] for u in range(width)], axis=1)

    def tile_pv(c, j, width, p):
        if VH == G:
            return jnp.dot(values_t(c, j, width), p, preferred_element_type=F32)
        return jnp.concatenate([jnp.dot(values_t(c * hc + h, j, width), p[:, h * TQ:(h + 1) * TQ],
                                        preferred_element_type=F32) for h in range(hc)], axis=1)

    _attend(tile_scores, tile_pv, 0, qi * own, acc_scr, m_scr, own_width=own)
    heads = _normalised_heads(acc_scr, TQ)
    for j in range(P // 2):
        o_ref[0, :, j * LANES:(j + 1) * LANES] = _head_pair_rows(heads[2 * j], heads[2 * j + 1]).astype(o_ref.dtype)


def _flash(q_t, k, v_t, out_dtype, mask_block, TQ=512):
    B, P, _, S = q_t.shape
    KH, VH, TK = k.shape[1], v_t.shape[1], v_t.shape[4]
    TQ = max(min(TQ, S), TK)
    assert v_t.shape == (B, VH, S // TK, VT_ROWS, TK) and k.shape == (B, KH, S, LANES) and TQ % TK == 0
    return pl.pallas_call(
        functools.partial(_flash_kernel, TQ=TQ, mask_block=mask_block),
        grid=(B, S // TQ),
        in_specs=[pl.BlockSpec((1, P, AUG_W, TQ), lambda b, i: (b, 0, 0, i)),
                  pl.BlockSpec((1, KH, S, LANES), lambda b, i: (b, 0, 0, 0)),
                  pl.BlockSpec((1, VH, S // TK, VT_ROWS, TK), lambda b, i: (b, 0, 0, 0, 0))],
        out_specs=pl.BlockSpec((1, TQ, P * HEAD_DIM), lambda b, i: (b, i, 0)),
        out_shape=jax.ShapeDtypeStruct((B, S, P * HEAD_DIM), out_dtype),
        scratch_shapes=[pltpu.VMEM((KH, VT_ROWS, P // KH * TQ), F32), pltpu.VMEM((KH, 1, P // KH * TQ), F32)],
        compiler_params=_cparams(2),
        name="flash",
    )(q_t, k, v_t)


def _window_kernel(qt_ref, k_ref, vt_ref, oc_ref, os_ref, sm_ref, y_ref, acc_scr, m_scr, *, T):
    R = ATT_HEADS
    TK = vt_ref.shape[3]
    own = T // TK
    qi = pl.program_id(1)
    q_cat = jnp.concatenate([qt_ref[0, h, LANES:AUG_W, :] for h in range(R)], axis=1)

    def tile_scores(c, j, width, last):
        rows = width * TK
        kt = k_ref[0, pl.ds(pl.multiple_of(j * TK, TK), rows), :]
        s_t = jnp.dot(kt, q_cat, preferred_element_type=F32)
        tq = qi * T + _iota((rows, R * T), 1) % T
        key = j * TK + _iota((rows, R * T), 0)
        return jnp.where((key <= tq) & (key > tq - WINDOW), s_t, NEG)

    def tile_pv(c, j, width, p):
        v_t = jnp.concatenate([vt_ref[0, j + u] for u in range(width)], axis=1)
        return jnp.dot(v_t, p, preferred_element_type=F32)

    first = jnp.maximum(qi * own - (WINDOW + TK - 1) // TK, 0)
    _attend(tile_scores, tile_pv, first, qi * own, acc_scr, m_scr, own_width=own)
    sig = jax.nn.sigmoid(sm_ref[0])
    lane = _iota((T, LANES), 1)
    heads = _normalised_heads(acc_scr, T)
    for j in range(R // 2):
        o_w = _head_pair_rows(heads[2 * j], heads[2 * j + 1])

        def gate(branch):
            c = SM_G + branch * R + 2 * j
            return jnp.where(lane < HEAD_DIM, sig[:, c:c + 1], sig[:, c + 1:c + 2])
        sl = slice(j * LANES, (j + 1) * LANES)
        y_ref[0, :, sl] = (gate(0) * oc_ref[0, :, sl] + gate(1) * os_ref[0, :, sl] + gate(2) * o_w).astype(y_ref.dtype)


def _window_combine(q_t, kw, vw_t, o_c, o_s, p3, T=512):
    B, R, _, S = q_t.shape
    TK = vw_t.shape[3]
    T = max(min(T, S), TK)
    packed = pl.BlockSpec((1, T, ATT_WIDTH), lambda b, i: (b, i, 0))
    return pl.pallas_call(
        functools.partial(_window_kernel, T=T),
        grid=(B, S // T),
        in_specs=[pl.BlockSpec((1, R, AUG_W, T), lambda b, i: (b, 0, 0, i)),
                  pl.BlockSpec((1, S, LANES), lambda b, i: (b, 0, 0)),
                  pl.BlockSpec((1, S // TK, VT_ROWS, TK), lambda b, i: (b, 0, 0, 0)),
                  packed, packed,
                  pl.BlockSpec((1, T, LANES), lambda b, i: (b, i, COL_SM // LANES))],
        out_specs=packed,
        out_shape=jax.ShapeDtypeStruct((B, S, ATT_WIDTH), BF16),
        scratch_shapes=[pltpu.VMEM((1, VT_ROWS, R * T), F32), pltpu.VMEM((1, 1, R * T), F32)],
        compiler_params=_cparams(2),
        name="nsa_window",
    )(q_t, kw, vw_t, o_c, o_s, p3)


def _compress_kernel(t_ref, pea_ref, peb_ref, w1a_ref, w1b_ref, w2_ref, kg_ref, kc_ref, vct_ref, t_scr):
    n = t_scr.shape[0]
    for r in range(CMP_STRIDE):
        t_scr[:, r * LANES:(r + 1) * LANES] = t_ref[0, pl.ds(r, n, stride=CMP_STRIDE), :]
    t = t_scr[...]
    a = jnp.dot((t + pea_ref[...]).astype(BF16), w1a_ref[...], preferred_element_type=F32)
    b = jnp.dot((t + peb_ref[...]).astype(BF16), w1b_ref[...], preferred_element_type=F32)
    hid = a + pltpu.roll(b, n - 1, 0)
    hid = hid * jax.nn.sigmoid(hid)
    kv = jnp.dot(hid.astype(BF16), w2_ref[...], preferred_element_type=F32)
    lane = _iota(kv.shape, 1)
    ms = jnp.sum(jnp.where(lane < HEAD_DIM, kv * kv, 0.0), axis=-1, keepdims=True) * (1.0 / HEAD_DIM)
    kc_ref[0] = jnp.where(lane < HEAD_DIM, kv * lax.rsqrt(ms + NORM_EPS) * kg_ref[...], 0.0).astype(kc_ref.dtype)
    vct_ref[0] = kv.T[HEAD_DIM:, :].astype(vct_ref.dtype)


def _compress(p3, pea, peb, w1a, w1b, w2, kg):
    B, S, _ = p3.shape
    n, W = S // CMP_STRIDE, CMP_STRIDE * LANES
    return pl.pallas_call(
        _compress_kernel,
        grid=(B,),
        in_specs=[pl.BlockSpec((1, S, LANES), lambda b: (b, 0, COL_NKV // LANES)),
                  _const_spec((1, W)), _const_spec((1, W)),
                  _const_spec(w1a.shape), _const_spec(w1b.shape), _const_spec(w2.shape),
                  _const_spec((1, LANES))],
        out_specs=[pl.BlockSpec((1, n, LANES), lambda b: (b, 0, 0)),
                   pl.BlockSpec((1, HEAD_DIM, n), lambda b: (b, 0, 0))],
        out_shape=[jax.ShapeDtypeStruct((B, n, LANES), BF16), jax.ShapeDtypeStruct((B, HEAD_DIM, n), BF16)],
        scratch_shapes=[pltpu.VMEM((n, W), F32)],
        compiler_params=_cparams(1),
        name="nsa_compress",
    )(p3, pea, peb, w1a, w1b, w2, kg)


def _nsa_prep_kernel(q_ref, kv_ref, cos_ref, sin_ref, cost_ref, sint_ref, qg_ref, kg_ref, kc_ref, vct_ref, ovt_ref,
                     oc_ref, qt_ref, ks_ref, kw_ref, vst_ref, vwt_ref, kmax_scr, *, n_sel):
    i = pl.program_id(1)
    T = q_ref.shape[1]
    nc = kc_ref.shape[1]

    @pl.when(i == 0)
    def _():
        kmax_scr[...] = jnp.zeros_like(kmax_scr)

    cos, sin = cos_ref[...], sin_ref[...]
    lane = _iota((T, LANES), 1)
    row = _iota((LANES, T), 0)
    tq = i * T + _iota((1, T), 1)

    qn_t = _rms_row_groups(q_ref[0].T, HEAD_DIM) * qg_ref[...]
    qr_t = _rope_t(qn_t, cost_ref[...], sint_ref[...])
    zeros_half = jnp.zeros((HEAD_DIM, T), F32)

    kc, vc_t = kc_ref[0], vct_ref[0]
    valid = _iota((nc, T), 0) * CMP_STRIDE + (CMP_LEN - 1) <= tq
    psum = jnp.zeros((nc, T), F32)
    o_heads = []
    for h in range(ATT_HEADS):
        qh = jnp.concatenate([qn_t[h * HEAD_DIM:(h + 1) * HEAD_DIM] * SCALE, zeros_half], axis=0).astype(BF16)
        s = jnp.where(valid, jnp.dot(kc, qh, preferred_element_type=F32), NEG)
        e = jnp.exp(s - jnp.max(s, axis=0, keepdims=True))
        p = jnp.where(valid, e / jnp.sum(e, axis=0, keepdims=True), 0.0)
        o_heads.append(jnp.dot(vc_t, p.astype(BF16), preferred_element_type=F32))
        psum = psum + p
    for j in range(ATT_HEADS // 2):
        oc_ref[0, :, j * LANES:(j + 1) * LANES] = _head_pair_rows(o_heads[2 * j], o_heads[2 * j + 1])

    imp = jnp.dot(ovt_ref[...], psum, precision=_HI, preferred_element_type=F32)
    blk_q = tq // SEL_BLOCK
    causal_blk = row <= blk_q
    forced = causal_blk & ((row == 0) | (row >= blk_q - 1))
    imp = jnp.where(forced, BIG, jnp.where(causal_blk, imp, NEG))
    imp = jnp.where(row < n_sel, imp, -jnp.inf)
    sel = _top_k_mask_t(imp, min(SEL_TOPK, n_sel))
    bias = jnp.where(sel > 0, 0.0, NEG)

    cos_k = jnp.where(lane < HEAD_DIM, cos[:, :LANES], 1.0)
    sin_k = jnp.where(lane < HEAD_DIM, sin[:, :LANES], 0.0)

    def key_pair(x, gain, idx):
        ms = jnp.sum(jnp.where(lane < HEAD_DIM, x * x, 0.0), axis=-1, keepdims=True) * (1.0 / HEAD_DIM)
        kn = _rope(x * lax.rsqrt(ms + NORM_EPS) * gain, cos_k, sin_k)
        kb = jnp.where(lane < HEAD_DIM, kn, 0.0).astype(BF16).astype(F32)
        return kb, _value_tile_t(x.T[HEAD_DIM:, :]), _running_max(kmax_scr, idx, _group_norms(kb, LANES)[0])

    ks, vst_ref[0, 0], ks_max = key_pair(kv_ref[0, :, LANES:2 * LANES], kg_ref[0:1, :], 0)
    kw, vwt_ref[0, 0], kw_max = key_pair(kv_ref[0, :, 2 * LANES:3 * LANES], kg_ref[1:2, :], 1)
    ks_ref[0] = ks.astype(BF16)
    kw_ref[0] = jnp.where(lane == HEAD_DIM, 1.0, kw).astype(BF16)

    q2_t = (qr_t * SCALE_LOG2).astype(BF16).astype(F32)
    row_h = _iota((HEAD_DIM, T), 0)
    for h, q_norm in enumerate(_row_group_norms(q2_t, HEAD_DIM)):
        qt_ref[0, h, 0:LANES, :] = (bias - q_norm * (ks_max * BOUND_SLACK)).astype(BF16)
        shift_rows = jnp.where(row_h == 0, -q_norm * (kw_max * BOUND_SLACK), 0.0)
        qt_ref[0, h, LANES:AUG_W, :] = jnp.concatenate([q2_t[h * HEAD_DIM:(h + 1) * HEAD_DIM], shift_rows],
                                                       axis=0).astype(BF16)


def _nsa_prep(p3, tables, qg_t, kg2, kc, vc_t, overlap_t, T=256):
    B, S, _ = p3.shape
    T = min(T, S)
    nc = kc.shape[1]
    n_sel = S // SEL_BLOCK
    tab = pl.BlockSpec((T, ATT_WIDTH), lambda b, i: (i, 0))
    tab_t = pl.BlockSpec((ATT_WIDTH, T), lambda b, i: (0, i))
    v_t = pl.BlockSpec((1, 1, VT_ROWS, T), lambda b, i: (b, i, 0, 0))
    v_t_shape = jax.ShapeDtypeStruct((B, S // T, VT_ROWS, T), BF16)
    return pl.pallas_call(
        functools.partial(_nsa_prep_kernel, n_sel=n_sel),
        grid=(B, S // T),
        in_specs=[pl.BlockSpec((1, T, ATT_WIDTH), lambda b, i: (b, i, COL_NQ // ATT_WIDTH)),
                  pl.BlockSpec((1, T, 3 * LANES), lambda b, i: (b, i, COL_NKV // (3 * LANES))),
                  tab, tab, tab_t, tab_t,
                  _const_spec((ATT_WIDTH, T)), _const_spec((2, LANES)),
                  pl.BlockSpec((1, nc, LANES), lambda b, i: (b, 0, 0)),
                  pl.BlockSpec((1, HEAD_DIM, nc), lambda b, i: (b, 0, 0)),
                  _const_spec((LANES, nc))],
        out_specs=[pl.BlockSpec((1, T, ATT_WIDTH), lambda b, i: (b, i, 0)),
                   pl.BlockSpec((1, ATT_HEADS, AUG_W, T), lambda b, i: (b, 0, 0, i)),
                   pl.BlockSpec((1, T, LANES), lambda b, i: (b, i, 0)),
                   pl.BlockSpec((1, T, LANES), lambda b, i: (b, i, 0)),
                   v_t, v_t],
        out_shape=[jax.ShapeDtypeStruct((B, S, ATT_WIDTH), F32),
                   jax.ShapeDtypeStruct((B, ATT_HEADS, AUG_W, S), BF16),
                   jax.ShapeDtypeStruct((B, S, LANES), BF16),
                   jax.ShapeDtypeStruct((B, S, LANES), BF16),
                   v_t_shape, v_t_shape],
        scratch_shapes=[pltpu.VMEM((2, 8, LANES), F32)],
        compiler_params=_cparams(2),
        name="nsa_prep",
    )(p3, p3, *tables, qg_t, kg2, kc, vc_t, overlap_t)


def _out_ffn_kernel(x_ref, ym_ref, yb_ref, yn_ref, wo_ref, g_ref, w1_ref, w2_ref, o_ref, *, fc):
    mix = jnp.concatenate([ym_ref[...], yb_ref[...], yn_ref[...]], axis=1)
    x = x_ref[...] + jnp.dot(mix, wo_ref[...], preferred_element_type=F32)
    hb = (x * lax.rsqrt(jnp.mean(x * x, axis=-1, keepdims=True) + NORM_EPS) * g_ref[...]).astype(BF16)
    acc = x
    for c in range(w1_ref.shape[1] // fc):
        u = jnp.maximum(jnp.dot(hb, w1_ref[:, c * fc:(c + 1) * fc], preferred_element_type=F32), 0.0)
        acc = acc + jnp.dot((u * u).astype(BF16), w2_ref[c * fc:(c + 1) * fc, :], preferred_element_type=F32)
    o_ref[...] = acc


def _out_ffn(x2d, y_m, y_b, y_n, wo, g, w1, w2, tm=512, fc=1024):
    M, D = x2d.shape
    tm = min(tm, M)
    rows = lambda w: pl.BlockSpec((tm, w), lambda i: (i, 0))
    resident = lambda a: pl.BlockSpec(a.shape, lambda i: (0,) * a.ndim, pipeline_mode=pl.Buffered(1))
    return pl.pallas_call(
        functools.partial(_out_ffn_kernel, fc=fc),
        grid=(M // tm,),
        in_specs=[rows(D), rows(M_WIDTH), rows(ATT_WIDTH), rows(ATT_WIDTH),
                  resident(wo), _const_spec((1, D)), resident(w1), resident(w2)],
        out_specs=rows(D),
        out_shape=jax.ShapeDtypeStruct((M, D), F32),
        compiler_params=_cparams(1),
        name="out_ffn",
    )(x2d, y_m, y_b, y_n, wo, g, w1, w2)


def _rope_tables(S):
    inv_freq = jnp.exp(-math.log(ROPE_THETA) * jnp.arange(ROT_HALF, dtype=F32) * (2.0 / ROT_DIM))
    ang = jnp.arange(S, dtype=F32)[:, None] * inv_freq[None, :]
    cos, sin = jnp.cos(ang), jnp.sin(ang)
    rest = HEAD_DIM - ROT_DIM
    cos64 = jnp.concatenate([cos, cos, jnp.ones((S, rest), F32)], axis=1)
    sin64 = jnp.concatenate([-sin, sin, jnp.zeros((S, rest), F32)], axis=1)
    return jnp.tile(cos64, (1, ATT_HEADS)), jnp.tile(sin64, (1, ATT_HEADS))


def _overlap_matrix(nc):
    c_start = np.arange(nc)[:, None] * CMP_STRIDE
    s_start = np.arange(LANES)[None, :] * SEL_BLOCK
    return jnp.asarray(((c_start < s_start + SEL_BLOCK) & (c_start + CMP_LEN > s_start)).astype(np.float32))


def _compress_weights(pe, w1, w2):
    half = CMP_LEN // 2
    pe_r = jnp.concatenate([pe[0], pe[1]], axis=-1)
    pea = pe_r[:half].reshape(1, half * LANES)
    peb = pe_r[half:].reshape(1, half * LANES)
    w1r = w1.reshape(2, CMP_LEN, HEAD_DIM, CMP_HIDDEN)
    z = jnp.zeros_like(w1r[0])
    wk = jnp.concatenate([w1r[0], z], axis=-1)
    wv = jnp.concatenate([z, w1r[1]], axis=-1)
    wboth = jnp.concatenate([wk, wv], axis=1)
    w1a = wboth[:half].reshape(half * LANES, 2 * CMP_HIDDEN).astype(BF16)
    w1b = wboth[half:].reshape(half * LANES, 2 * CMP_HIDDEN).astype(BF16)
    z2 = jnp.zeros_like(w2[0])
    w2bd = jnp.concatenate([jnp.concatenate([w2[0], z2], axis=1),
                            jnp.concatenate([z2, w2[1]], axis=1)], axis=0).astype(BF16)
    return pea, peb, w1a, w1b, w2bd


def _pad_lanes(v, width=LANES):
    return jnp.concatenate([v, jnp.zeros((width - v.shape[0],), v.dtype)])[None, :]


def kernel(x, w_in, b_if, conv_qk, m_norm, moba_qk_norm, nsa_q_norm, nsa_k_norm, cmp_pe, cmp_w1, cmp_w2,
           w_out, norm_mix, norm_ffn, w_ff1, w_ff2):
    B, S, D = x.shape
    depth = w_in.shape[0]
    cos, sin = _rope_tables(S)
    tables = (cos, sin, cos.T, sin.T)
    overlap_t = _overlap_matrix(S // CMP_STRIDE).T
    x2d = x.reshape(B * S, D)
    for l in range(depth):
        p3 = _proj(x2d, norm_mix[l][None, :], w_in, l).reshape(B, S, P_W)

        y_m = _mlstm(p3, conv_qk[l], _pad_lanes(b_if[l]), m_norm[l][None, :])

        tile_g = lambda g: jnp.tile(g, ATT_HEADS)[None, :]
        rows_g = lambda g: jnp.broadcast_to(jnp.tile(g, ATT_HEADS)[:, None], (ATT_WIDTH, min(MOBA_BLOCK, S)))
        qt_b, kp_b, vt_b = _moba_prep(p3, tables, rows_g(moba_qk_norm[l, 0]), tile_g(moba_qk_norm[l, 1]))
        y_b = _flash(qt_b, kp_b, vt_b, BF16, MOBA_BLOCK)

        pea, peb, w1a, w1b, w2bd = _compress_weights(cmp_pe[l], cmp_w1[l], cmp_w2[l])
        kc, vc_t = _compress(p3, pea, peb, w1a, w1b, w2bd, _pad_lanes(nsa_k_norm[l, 0]))
        kg2 = jnp.concatenate([_pad_lanes(nsa_k_norm[l, 1]), _pad_lanes(nsa_k_norm[l, 2])], axis=0)
        o_c, qt_n, ks, kw, vs_t, vw_t = _nsa_prep(p3, tables, rows_g(nsa_q_norm[l]), kg2, kc, vc_t, overlap_t)
        o_s = _flash(qt_n, ks[:, None], vs_t[:, None], F32, SEL_BLOCK)
        y_n = _window_combine(qt_n, kw, vw_t, o_c, o_s, p3)

        x2d = _out_ffn(x2d, y_m.reshape(B * S, M_WIDTH), y_b.reshape(B * S, ATT_WIDTH), y_n.reshape(B * S, ATT_WIDTH),
                       w_out[l].astype(BF16), norm_ffn[l][None, :], w_ff1[l].astype(BF16), w_ff2[l].astype(BF16))
    return x2d.reshape(B, S, D)
```

```python
import functools
import math

import jax
import jax.numpy as jnp
import numpy as np
from jax import lax
from jax.experimental import pallas as pl
from jax.experimental.pallas import tpu as pltpu

F32 = jnp.float32
BF16 = jnp.bfloat16

HEAD_DIM = 64
M_HEADS = 4
M_HEAD_DIM = 128
M_WIDTH = M_HEADS * M_HEAD_DIM
CONV_W = 4
ATT_HEADS = 4
ATT_WIDTH = ATT_HEADS * HEAD_DIM
MOBA_BLOCK = 256
MOBA_TOPK = 3
CMP_LEN = 32
CMP_STRIDE = 16
CMP_HIDDEN = 128
SEL_BLOCK = 64
SEL_TOPK = 16
WINDOW = 512
ROPE_THETA = 500000.0
ROT_DIM = HEAD_DIM // 4
ROT_HALF = ROT_DIM // 2
NORM_EPS = 1e-6
NEG = -1e30
BIG = 1e9
SCALE = HEAD_DIM ** -0.5
SCALE_LOG2 = SCALE * math.log2(math.e)
BOUND_SLACK = 1.0 + 2.0 ** -7
L_TINY = 2.0 ** -100

SUBLANES = 8
LANES = 128
AUG_W = 2 * LANES
VT_ROWS = HEAD_DIM + 16
ATTEND_GROUP = 4
VMEM_LIMIT = 56 * 1024 * 1024

COL_MQK, COL_MV, COL_MO = 0, 1024, 1536
COL_BQ, COL_BK, COL_BV = 2048, 2304, 2560
COL_NQ, COL_NKV, COL_SM = 2816, 3072, 3456
P_W = 3584
SM_I, SM_F, SM_G = 0, 4, 8
_W_ATT0 = 4 * M_WIDTH + 2 * M_HEADS
_W_ATT1 = _W_ATT0 + 4 * ATT_WIDTH + 6 * HEAD_DIM
W_IN_SECTIONS = ((0, 4 * M_WIDTH), (_W_ATT0, _W_ATT1 - _W_ATT0), (4 * M_WIDTH, 2 * M_HEADS), (_W_ATT1, 3 * ATT_HEADS))

_NT = (((1,), (1,)), ((), ()))
_HI = lax.Precision.HIGHEST


def _iota(shape, dim):
    return lax.broadcasted_iota(jnp.int32, shape, dim)


def _cparams(n_axes):
    return pltpu.CompilerParams(dimension_semantics=("arbitrary",) * n_axes,
                                vmem_limit_bytes=VMEM_LIMIT)


def _const_spec(shape):
    nd = len(shape)
    return pl.BlockSpec(shape, lambda *_: (0,) * nd)


def _rms_groups(x, width):
    T, W = x.shape
    x2 = x * x
    lane = _iota((T, W), 1)
    scale = None
    for h in range(W // width):
        r = lax.rsqrt(jnp.mean(x2[:, h * width:(h + 1) * width], axis=-1, keepdims=True) + NORM_EPS)
        scale = r if scale is None else jnp.where(lane >= h * width, r, scale)
    return x * scale


def _rope(x, cos, sin):
    W = x.shape[1]
    lane = _iota(x.shape, 1) % HEAD_DIM
    up = pltpu.roll(x, W - ROT_HALF, 1)
    dn = pltpu.roll(x, ROT_HALF, 1)
    return x * cos + jnp.where(lane < ROT_HALF, up, dn) * sin


def _group_norms(x, width):
    x2 = x * x
    return [jnp.sqrt(jnp.sum(x2[:, h * width:(h + 1) * width], axis=-1, keepdims=True))
            for h in range(x.shape[1] // width)]


def _running_max(scr, idx, col):
    new = jnp.maximum(scr[idx][0:1, 0:1], jnp.max(col, axis=0, keepdims=True))
    scr[idx] = jnp.broadcast_to(new, scr.shape[1:])
    return new


def _value_tile_t(v_t):
    extra = jnp.where(_iota((VT_ROWS - HEAD_DIM, v_t.shape[1]), 0) == 0, 1.0, 0.0)
    return jnp.concatenate([v_t, extra], axis=0).astype(BF16)


def _rms_row_groups(x_t, width):
    parts = []
    for h in range(x_t.shape[0] // width):
        g = x_t[h * width:(h + 1) * width]
        parts.append(g * lax.rsqrt(jnp.mean(g * g, axis=0, keepdims=True) + NORM_EPS))
    return jnp.concatenate(parts, axis=0)


def _rope_t(x_t, cos_t, sin_t):
    W = x_t.shape[0]
    row = _iota(x_t.shape, 0) % HEAD_DIM
    up = pltpu.roll(x_t, W - ROT_HALF, 0)
    dn = pltpu.roll(x_t, ROT_HALF, 0)
    return x_t * cos_t + jnp.where(row < ROT_HALF, up, dn) * sin_t


def _row_group_norms(x_t, width):
    return [jnp.sqrt(jnp.sum(jnp.square(x_t[h * width:(h + 1) * width]), axis=0, keepdims=True))
            for h in range(x_t.shape[0] // width)]


def _top_k_mask_t(vals, k, rank_limit=None):
    row = _iota(vals.shape, 0)
    sel = jnp.zeros(vals.shape, jnp.int32)
    for r in range(k):
        mx = jnp.max(vals, axis=0, keepdims=True)
        idx = jnp.min(jnp.where(vals == mx, row, LANES), axis=0, keepdims=True)
        pick = row == idx
        mark = 1 if rank_limit is None else jnp.where(rank_limit > r, 1, 0)
        sel = jnp.where(pick, mark, sel)
        vals = jnp.where(pick, -jnp.inf, vals)
    return sel


def _proj_kernel(x_ref, g_ref, w_ref, o_ref, w_scr):
    @pl.when(pl.program_id(0) == 0)
    def _():
        dst = 0
        for src, width in W_IN_SECTIONS:
            w_scr[:, dst:dst + width] = w_ref[0, :, src:src + width].astype(BF16)
            dst += width
        w_scr[:, dst:] = jnp.zeros((w_scr.shape[0], w_scr.shape[1] - dst), BF16)

    x = x_ref[...]
    h = x * lax.rsqrt(jnp.mean(x * x, axis=-1, keepdims=True) + NORM_EPS) * g_ref[...]
    o_ref[...] = jnp.dot(h.astype(BF16), w_scr[...], preferred_element_type=F32)


def _proj(x2d, g, w_all, layer, tm=512):
    M, D = x2d.shape
    N = P_W
    assert w_all.shape[2] == sum(width for _, width in W_IN_SECTIONS)
    return pl.pallas_call(
        _proj_kernel,
        grid=(M // tm,),
        in_specs=[pl.BlockSpec((tm, D), lambda i: (i, 0)), _const_spec((1, D)),
                  pl.BlockSpec((1, D, w_all.shape[2]), lambda i: (layer, 0, 0), pipeline_mode=pl.Buffered(1))],
        out_specs=pl.BlockSpec((tm, N), lambda i: (i, 0)),
        out_shape=jax.ShapeDtypeStruct((M, N), F32),
        scratch_shapes=[pltpu.VMEM((D, N), BF16)],
        compiler_params=_cparams(1),
        name="in_proj",
    )(x2d, g, w_all)


def _mlstm_kernel(qk_ref, v_ref, o_ref, sm_ref, cw_ref, b_ref, g_ref, out_ref, xbuf, c_scr, m_scr, *, Lc):
    W2 = 2 * M_WIDTH
    D = M_HEAD_DIM

    @pl.when(pl.program_id(1) == 0)
    def _():
        xbuf[0:SUBLANES, :] = jnp.zeros((SUBLANES, W2), F32)
        c_scr[...] = jnp.zeros_like(c_scr)
        m_scr[...] = jnp.zeros_like(m_scr)

    x = qk_ref[0]
    xbuf[SUBLANES:SUBLANES + Lc, :] = x
    first = SUBLANES - CONV_W + 1
    y = cw_ref[0:1, :] * xbuf[first:first + Lc, :]
    for j in range(1, CONV_W):
        y = y + cw_ref[j:j + 1, :] * xbuf[first + j:first + j + Lc, :]
    xbuf[0:SUBLANES, :] = x[Lc - SUBLANES:Lc, :]
    qk = y * jax.nn.sigmoid(y)

    gi = sm_ref[0] + b_ref[...]
    lane = _iota((Lc, LANES), 1)
    lsig = jnp.minimum(gi, 0.0) - jnp.log1p(jnp.exp(-jnp.abs(gi)))
    gates = jnp.where(lane < SM_F, gi, jnp.where(lane < SM_F + M_HEADS, lsig, 0.0))
    gates_t = gates.T
    ri, ci = _iota((Lc, Lc), 0), _iota((Lc, Lc), 1)
    causal = ci <= ri
    b_col = jnp.dot(causal.astype(F32), gates, precision=_HI, preferred_element_type=F32)
    b_row = jnp.dot(gates_t[0:2 * M_HEADS, :], (ri <= ci).astype(F32), precision=_HI,
                    preferred_element_type=F32)
    ones_col = jnp.where(lane == 0, 1.0, 0.0)

    for h in range(M_HEADS):
        q = qk[:, h * D:(h + 1) * D]
        k = qk[:, M_WIDTH + h * D:M_WIDTH + (h + 1) * D] * (D ** -0.5)
        v_aug = jnp.concatenate([v_ref[0, :, h * D:(h + 1) * D], ones_col], axis=1).astype(BF16)
        bc = b_col[:, SM_F + h:SM_F + h + 1]
        br = b_row[SM_F + h:SM_F + h + 1, :]
        li_r = gates_t[SM_I + h:SM_I + h + 1, :]
        li_c = gates[:, SM_I + h:SM_I + h + 1]
        m_prev = m_scr[h][0:1, 0:1]

        dmat = jnp.where(causal, bc - br + li_r, -jnp.inf)
        inter = bc + m_prev
        m_t = jnp.maximum(inter, jnp.max(dmat, axis=-1, keepdims=True))
        w_intra = jnp.exp(dmat - m_t)
        w_prev = jnp.exp(inter - m_t)
        qb = q.astype(BF16)
        s = lax.dot_general(qb, k.astype(BF16), _NT, preferred_element_type=F32) * w_intra
        tot = (jnp.dot(s.astype(BF16), v_aug, preferred_element_type=F32)
               + w_prev * jnp.dot(qb, c_scr[h].astype(BF16), preferred_element_type=F32))
        den = tot[:, D:D + 1]
        hh = tot[:, :D] / jnp.maximum(jnp.abs(den), jnp.exp(-m_t))

        b_last = bc[Lc - 1:Lc, :]
        g_c = b_last - bc + li_c
        m_new = jnp.maximum(b_last + m_prev, jnp.max(g_c, axis=0, keepdims=True))
        a = jnp.exp(b_last + m_prev - m_new)
        kw_t = (k * jnp.exp(g_c - m_new)).T.astype(BF16)
        c_scr[h] = a * c_scr[h] + jnp.dot(kw_t, v_aug, preferred_element_type=F32)
        m_scr[h] = jnp.broadcast_to(m_new, (SUBLANES, LANES))

        hn = hh * lax.rsqrt(jnp.mean(hh * hh, axis=-1, keepdims=True) + NORM_EPS) * g_ref[0:1, h * D:(h + 1) * D]
        out_ref[0, :, h * D:(h + 1) * D] = (jax.nn.sigmoid(o_ref[0, :, h * D:(h + 1) * D]) * hn).astype(BF16)


def _mlstm(p3, conv_w, b_sm, m_norm, Lc=256):
    B, S, _ = p3.shape
    Lc = min(Lc, S)
    kern = functools.partial(_mlstm_kernel, Lc=Lc)
    return pl.pallas_call(
        kern,
        grid=(B, S // Lc),
        in_specs=[
            pl.BlockSpec((1, Lc, 2 * M_WIDTH), lambda b, c: (b, c, COL_MQK // (2 * M_WIDTH))),
            pl.BlockSpec((1, Lc, M_WIDTH), lambda b, c: (b, c, COL_MV // M_WIDTH)),
            pl.BlockSpec((1, Lc, M_WIDTH), lambda b, c: (b, c, COL_MO // M_WIDTH)),
            pl.BlockSpec((1, Lc, LANES), lambda b, c: (b, c, COL_SM // LANES)),
            _const_spec((CONV_W, 2 * M_WIDTH)),
            _const_spec((1, LANES)),
            _const_spec((1, M_WIDTH)),
        ],
        out_specs=pl.BlockSpec((1, Lc, M_WIDTH), lambda b, c: (b, c, 0)),
        out_shape=jax.ShapeDtypeStruct((B, S, M_WIDTH), BF16),
        scratch_shapes=[
            pltpu.VMEM((Lc + SUBLANES, 2 * M_WIDTH), F32),
            pltpu.VMEM((M_HEADS, M_HEAD_DIM, 2 * M_HEAD_DIM), F32),
            pltpu.VMEM((M_HEADS, SUBLANES, LANES), F32),
        ],
        compiler_params=_cparams(2),
        name="mlstm",
    )(p3, p3, p3, p3, conv_w, b_sm, m_norm)


def _moba_prep_kernel(q_ref, k_ref, v_ref, cos_ref, sin_ref, cost_ref, sint_ref, qg_ref, kg_ref,
                      qt_ref, kp_ref, vt_ref, kmean_scr, kmax_scr):
    i = pl.program_id(1)
    T = q_ref.shape[1]

    @pl.when(i == 0)
    def _():
        kmean_scr[...] = jnp.zeros_like(kmean_scr)
        kmax_scr[...] = jnp.zeros_like(kmax_scr)

    kn = _rope(_rms_groups(k_ref[0], HEAD_DIM) * kg_ref[...], cos_ref[...], sin_ref[...])
    kb = kn.astype(BF16).astype(F32)
    k_norms = _group_norms(kb, HEAD_DIM)
    qn_t = _rope_t(_rms_row_groups(q_ref[0].T, HEAD_DIM) * qg_ref[...], cost_ref[...], sint_ref[...])
    q2_t = (qn_t * SCALE_LOG2).astype(BF16).astype(F32)
    q_norms = _row_group_norms(q2_t, HEAD_DIM)
    v_t = v_ref[0].T
    row_w = _iota((ATT_WIDTH, T), 0)
    row = _iota((LANES, T), 0)
    zeros_half = jnp.zeros((HEAD_DIM, T), F32)
    kmeans = kmean_scr[...]
    for h in range(ATT_HEADS):
        qm = jnp.where((row_w >= h * HEAD_DIM) & (row_w < (h + 1) * HEAD_DIM), qn_t, 0.0)
        gs = jnp.dot(kmeans, qm, precision=_HI, preferred_element_type=F32)
        sel = _top_k_mask_t(jnp.where(row < i, gs, NEG), MOBA_TOPK, rank_limit=i)
        bound = q_norms[h] * (_running_max(kmax_scr, h, k_norms[h]) * BOUND_SLACK)
        qt_ref[0, h, 0:LANES, :] = (jnp.where((sel > 0) | (row == i), 0.0, NEG) - bound).astype(BF16)
        q_h = q2_t[h * HEAD_DIM:(h + 1) * HEAD_DIM]
        qt_ref[0, h, LANES:AUG_W, :] = jnp.concatenate([zeros_half, q_h] if h % 2 else [q_h, zeros_half],
                                                       axis=0).astype(BF16)
        vt_ref[0, h, 0] = _value_tile_t(v_t[h * HEAD_DIM:(h + 1) * HEAD_DIM, :])
    for j in range(ATT_HEADS // 2):
        kp_ref[0, j] = kb[:, j * LANES:(j + 1) * LANES].astype(BF16)
    kmean_scr[pl.ds(i, 1), :] = jnp.mean(kn, axis=0, keepdims=True)


def _moba_prep(p3, tables, qg_t, kg):
    B, S, _ = p3.shape
    T = MOBA_BLOCK
    blk = lambda col: pl.BlockSpec((1, T, ATT_WIDTH), lambda b, i: (b, i, col // ATT_WIDTH))
    tab = pl.BlockSpec((T, ATT_WIDTH), lambda b, i: (i, 0))
    tab_t = pl.BlockSpec((ATT_WIDTH, T), lambda b, i: (0, i))
    return pl.pallas_call(
        _moba_prep_kernel,
        grid=(B, S // T),
        in_specs=[blk(COL_BQ), blk(COL_BK), blk(COL_BV), tab, tab, tab_t, tab_t,
                  _const_spec((ATT_WIDTH, T)), _const_spec((1, ATT_WIDTH))],
        out_specs=[pl.BlockSpec((1, ATT_HEADS, AUG_W, T), lambda b, i: (b, 0, 0, i)),
                   pl.BlockSpec((1, ATT_HEADS // 2, T, LANES), lambda b, i: (b, 0, i, 0)),
                   pl.BlockSpec((1, ATT_HEADS, 1, VT_ROWS, T), lambda b, i: (b, 0, i, 0, 0))],
        out_shape=[jax.ShapeDtypeStruct((B, ATT_HEADS, AUG_W, S), BF16),
                   jax.ShapeDtypeStruct((B, ATT_HEADS // 2, S, LANES), BF16),
                   jax.ShapeDtypeStruct((B, ATT_HEADS, S // T, VT_ROWS, T), BF16)],
        scratch_shapes=[pltpu.VMEM((LANES, ATT_WIDTH), F32), pltpu.VMEM((ATT_HEADS, SUBLANES, LANES), F32)],
        compiler_params=_cparams(2),
        name="moba_prep",
    )(p3, p3, p3, *tables, qg_t, kg)


def _attend(tile_scores, tile_pv, lo, hi, acc_scr, m_scr, own_width=1):
    n_chain = acc_scr.shape[0]

    def run(exact):
        acc_scr[...] = jnp.zeros_like(acc_scr)
        if exact:
            m_scr[...] = jnp.full(m_scr.shape, NEG, F32)

        def tile(j, width, last):
            for c in range(n_chain):
                s_t = tile_scores(c, j, width, last)
                if exact:
                    m_prev = m_scr[c]
                    m_new = jnp.maximum(m_prev, jnp.max(s_t, axis=0, keepdims=True))
                    acc_scr[c] = (acc_scr[c] * jnp.exp2(m_prev - m_new)
                                  + tile_pv(c, j, width, jnp.exp2(s_t - m_new).astype(BF16)))
                    m_scr[c] = m_new
                else:
                    acc_scr[c] += tile_pv(c, j, width, jnp.exp2(s_t).astype(BF16))

        group = 1 if exact else ATTEND_GROUP
        n = hi - lo

        def body(i, carry):
            tile(lo + group * i, group, False)
            return carry

        lax.fori_loop(0, n // group, body, 0)
        start = lo + (n // group) * group
        merged = own_width if own_width < group else 0
        b = group // 2
        while b:
            if b != merged:
                take = (n & b) != 0

                @pl.when(take)
                def _(start=start, b=b):
                    tile(start, b, False)

                start = start + jnp.where(take, b, 0)
            b //= 2

        if merged:
            take = (n & merged) != 0

            @pl.when(take)
            def _():
                tile(hi - merged, merged + own_width, True)

            @pl.when(jnp.logical_not(take))
            def _():
                tile(hi, own_width, True)
        else:
            tile(hi, own_width, True)

    run(False)
    l_min = jnp.min(acc_scr[:, HEAD_DIM:HEAD_DIM + 1, :])

    @pl.when(jnp.logical_not(l_min >= L_TINY))
    def _():
        run(True)


def _head_pair_rows(o_a, o_b):
    return jnp.concatenate([o_a, o_b], axis=0).T


def _normalised_heads(acc_scr, T):
    heads = []
    for c in range(acc_scr.shape[0]):
        o = acc_scr[c, 0:HEAD_DIM, :] / acc_scr[c, HEAD_DIM:HEAD_DIM + 1, :]
        heads += [o[:, h * T:(h + 1) * T] for h in range(o.shape[1] // T)]
    return heads


def _flash_kernel(qt_ref, k_ref, vt_ref, o_ref, acc_scr, m_scr, *, TQ, mask_block):
    qi = pl.program_id(1)
    P, G, VH = qt_ref.shape[1], k_ref.shape[1], vt_ref.shape[1]
    TK = vt_ref.shape[4]
    own = TQ // TK
    hc = P // G
    N = hc * TQ
    q_cat = [jnp.concatenate([qt_ref[0, c * hc + h] for h in range(hc)], axis=1) for c in range(G)]
    def tile_scores(c, j, width, last):
        rows = width * TK
        lane, row = _iota((rows, LANES), 1), _iota((rows, LANES), 0)
        onehot = jnp.where(lane == (j * TK + row) // mask_block, 1.0, 0.0).astype(BF16)
        kt = jnp.concatenate([onehot, k_ref[0, c, pl.ds(pl.multiple_of(j * TK, TK), rows), :]], axis=1)
        s_t = jnp.dot(kt, q_cat[c], preferred_element_type=F32)
        if last:
            key = _iota((rows, N), 0) - (rows - TQ)
            s_t = jnp.where(key <= _iota((rows, N), 1) % TQ, s_t, NEG)
        return s_t

    def values_t(v, j, width):
        return jnp.concatenate([vt_ref[0, v, j + u] for u in range(width)], axis=1)

    def tile_pv(c, j, width, p):
        if VH == G:
            return jnp.dot(values_t(c, j, width), p, preferred_element_type=F32)
        return jnp.concatenate([jnp.dot(values_t(c * hc + h, j, width), p[:, h * TQ:(h + 1) * TQ],
                                        preferred_element_type=F32) for h in range(hc)], axis=1)

    _attend(tile_scores, tile_pv, 0, qi * own, acc_scr, m_scr, own_width=own)
    heads = _normalised_heads(acc_scr, TQ)
    for j in range(P // 2):
        o_ref[0, :, j * LANES:(j + 1) * LANES] = _head_pair_rows(heads[2 * j], heads[2 * j + 1]).astype(o_ref.dtype)


def _flash(q_t, k, v_t, out_dtype, mask_block, TQ=512):
    B, P, _, S = q_t.shape
    KH, VH, TK = k.shape[1], v_t.shape[1], v_t.shape[4]
    TQ = max(min(TQ, S), TK)
    assert v_t.shape == (B, VH, S // TK, VT_ROWS, TK) and k.shape == (B, KH, S, LANES) and TQ % TK == 0
    return pl.pallas_call(
        functools.partial(_flash_kernel, TQ=TQ, mask_block=mask_block),
        grid=(B, S // TQ),
        in_specs=[pl.BlockSpec((1, P, AUG_W, TQ), lambda b, i: (b, 0, 0, i)),
                  pl.BlockSpec((1, KH, S, LANES), lambda b, i: (b, 0, 0, 0)),
                  pl.BlockSpec((1, VH, S // TK, VT_ROWS, TK), lambda b, i: (b, 0, 0, 0, 0))],
        out_specs=pl.BlockSpec((1, TQ, P * HEAD_DIM), lambda b, i: (b, i, 0)),
        out_shape=jax.ShapeDtypeStruct((B, S, P * HEAD_DIM), out_dtype),
        scratch_shapes=[pltpu.VMEM((KH, VT_ROWS, P // KH * TQ), F32), pltpu.VMEM((KH, 1, P // KH * TQ), F32)],
        compiler_params=_cparams(2),
        name="flash",
    )(q_t, k, v_t)


def _window_kernel(qt_ref, k_ref, vt_ref, oc_ref, os_ref, sm_ref, y_ref, acc_scr, m_scr, *, T):
    R = ATT_HEADS
    TK = vt_ref.shape[3]
    own = T // TK
    qi = pl.program_id(1)
    q_cat = jnp.concatenate([qt_ref[0, h, LANES:AUG_W, :] for h in range(R)], axis=1)

    def tile_scores(c, j, width, last):
        rows = width * TK
        kt = k_ref[0, pl.ds(pl.multiple_of(j * TK, TK), rows), :]
        s_t = jnp.dot(kt, q_cat, preferred_element_type=F32)
        tq = qi * T + _iota((rows, R * T), 1) % T
        key = j * TK + _iota((rows, R * T), 0)
        return jnp.where((key <= tq) & (key > tq - WINDOW), s_t, NEG)

    def tile_pv(c, j, width, p):
        v_t = jnp.concatenate([vt_ref[0, j + u] for u in range(width)], axis=1)
        return jnp.dot(v_t, p, preferred_element_type=F32)

    first = jnp.maximum(qi * own - (WINDOW + TK - 1) // TK, 0)
    _attend(tile_scores, tile_pv, first, qi * own, acc_scr, m_scr, own_width=own)
    sig = jax.nn.sigmoid(sm_ref[0])
    lane = _iota((T, LANES), 1)
    heads = _normalised_heads(acc_scr, T)
    for j in range(R // 2):
        o_w = _head_pair_rows(heads[2 * j], heads[2 * j + 1])

        def gate(branch):
            c = SM_G + branch * R + 2 * j
            return jnp.where(lane < HEAD_DIM, sig[:, c:c + 1], sig[:, c + 1:c + 2])
        sl = slice(j * LANES, (j + 1) * LANES)
        y_ref[0, :, sl] = (gate(0) * oc_ref[0, :, sl] + gate(1) * os_ref[0, :, sl] + gate(2) * o_w).astype(y_ref.dtype)


def _window_combine(q_t, kw, vw_t, o_c, o_s, p3, T=512):
    B, R, _, S = q_t.shape
    TK = vw_t.shape[3]
    T = max(min(T, S), TK)
    packed = pl.BlockSpec((1, T, ATT_WIDTH), lambda b, i: (b, i, 0))
    return pl.pallas_call(
        functools.partial(_window_kernel, T=T),
        grid=(B, S // T),
        in_specs=[pl.BlockSpec((1, R, AUG_W, T), lambda b, i: (b, 0, 0, i)),
                  pl.BlockSpec((1, S, LANES), lambda b, i: (b, 0, 0)),
                  pl.BlockSpec((1, S // TK, VT_ROWS, TK), lambda b, i: (b, 0, 0, 0)),
                  packed, packed,
                  pl.BlockSpec((1, T, LANES), lambda b, i: (b, i, COL_SM // LANES))],
        out_specs=packed,
        out_shape=jax.ShapeDtypeStruct((B, S, ATT_WIDTH), BF16),
        scratch_shapes=[pltpu.VMEM((1, VT_ROWS, R * T), F32), pltpu.VMEM((1, 1, R * T), F32)],
        compiler_params=_cparams(2),
        name="nsa_window",
    )(q_t, kw, vw_t, o_c, o_s, p3)


def _compress_kernel(t_ref, pea_ref, peb_ref, w1a_ref, w1b_ref, w2_ref, kg_ref, kc_ref, vct_ref, t_scr):
    n = t_scr.shape[0]
    for r in range(CMP_STRIDE):
        t_scr[:, r * LANES:(r + 1) * LANES] = t_ref[0, pl.ds(r, n, stride=CMP_STRIDE), :]
    t = t_scr[...]
    a = jnp.dot((t + pea_ref[...]).astype(BF16), w1a_ref[...], preferred_element_type=F32)
    b = jnp.dot((t + peb_ref[...]).astype(BF16), w1b_ref[...], preferred_element_type=F32)
    hid = a + pltpu.roll(b, n - 1, 0)
    hid = hid * jax.nn.sigmoid(hid)
    kv = jnp.dot(hid.astype(BF16), w2_ref[...], preferred_element_type=F32)
    lane = _iota(kv.shape, 1)
    ms = jnp.sum(jnp.where(lane < HEAD_DIM, kv * kv, 0.0), axis=-1, keepdims=True) * (1.0 / HEAD_DIM)
    kc_ref[0] = jnp.where(lane < HEAD_DIM, kv * lax.rsqrt(ms + NORM_EPS) * kg_ref[...], 0.0).astype(kc_ref.dtype)
    vct_ref[0] = kv.T[HEAD_DIM:, :].astype(vct_ref.dtype)


def _compress(p3, pea, peb, w1a, w1b, w2, kg):
    B, S, _ = p3.shape
    n, W = S // CMP_STRIDE, CMP_STRIDE * LANES
    return pl.pallas_call(
        _compress_kernel,
        grid=(B,),
        in_specs=[pl.BlockSpec((1, S, LANES), lambda b: (b, 0, COL_NKV // LANES)),
                  _const_spec((1, W)), _const_spec((1, W)),
                  _const_spec(w1a.shape), _const_spec(w1b.shape), _const_spec(w2.shape),
                  _const_spec((1, LANES))],
        out_specs=[pl.BlockSpec((1, n, LANES), lambda b: (b, 0, 0)),
                   pl.BlockSpec((1, HEAD_DIM, n), lambda b: (b, 0, 0))],
        out_shape=[jax.ShapeDtypeStruct((B, n, LANES), BF16), jax.ShapeDtypeStruct((B, HEAD_DIM, n), BF16)],
        scratch_shapes=[pltpu.VMEM((n, W), F32)],
        compiler_params=_cparams(1),
        name="nsa_compress",
    )(p3, pea, peb, w1a, w1b, w2, kg)


def _nsa_prep_kernel(q_ref, kv_ref, cos_ref, sin_ref, cost_ref, sint_ref, qg_ref, kg_ref, kc_ref, vct_ref, ovt_ref,
                     oc_ref, qt_ref, ks_ref, kw_ref, vst_ref, vwt_ref, kmax_scr, *, n_sel):
    i = pl.program_id(1)
    T = q_ref.shape[1]
    nc = kc_ref.shape[1]

    @pl.when(i == 0)
    def _():
        kmax_scr[...] = jnp.zeros_like(kmax_scr)

    cos, sin = cos_ref[...], sin_ref[...]
    lane = _iota((T, LANES), 1)
    row = _iota((LANES, T), 0)
    tq = i * T + _iota((1, T), 1)

    qn_t = _rms_row_groups(q_ref[0].T, HEAD_DIM) * qg_ref[...]
    qr_t = _rope_t(qn_t, cost_ref[...], sint_ref[...])
    zeros_half = jnp.zeros((HEAD_DIM, T), F32)

    kc, vc_t = kc_ref[0], vct_ref[0]
    valid = _iota((nc, T), 0) * CMP_STRIDE + (CMP_LEN - 1) <= tq
    psum = jnp.zeros((nc, T), F32)
    o_heads = []
    for h in range(ATT_HEADS):
        qh = jnp.concatenate([qn_t[h * HEAD_DIM:(h + 1) * HEAD_DIM] * SCALE, zeros_half], axis=0).astype(BF16)
        s = jnp.where(valid, jnp.dot(kc, qh, preferred_element_type=F32), NEG)
        e = jnp.exp(s - jnp.max(s, axis=0, keepdims=True))
        p = jnp.where(valid, e / jnp.sum(e, axis=0, keepdims=True), 0.0)
        o_heads.append(jnp.dot(vc_t, p.astype(BF16), preferred_element_type=F32))
        psum = psum + p
    for j in range(ATT_HEADS // 2):
        oc_ref[0, :, j * LANES:(j + 1) * LANES] = _head_pair_rows(o_heads[2 * j], o_heads[2 * j + 1])

    imp = jnp.dot(ovt_ref[...], psum, precision=_HI, preferred_element_type=F32)
    blk_q = tq // SEL_BLOCK
    causal_blk = row <= blk_q
    forced = causal_blk & ((row == 0) | (row >= blk_q - 1))
    imp = jnp.where(forced, BIG, jnp.where(causal_blk, imp, NEG))
    imp = jnp.where(row < n_sel, imp, -jnp.inf)
    sel = _top_k_mask_t(imp, min(SEL_TOPK, n_sel))
    bias = jnp.where(sel > 0, 0.0, NEG)

    cos_k = jnp.where(lane < HEAD_DIM, cos[:, :LANES], 1.0)
    sin_k = jnp.where(lane < HEAD_DIM, sin[:, :LANES], 0.0)

    def key_pair(x, gain, idx):
        ms = jnp.sum(jnp.where(lane < HEAD_DIM, x * x, 0.0), axis=-1, keepdims=True) * (1.0 / HEAD_DIM)
        kn = _rope(x * lax.rsqrt(ms + NORM_EPS) * gain, cos_k, sin_k)
        kb = jnp.where(lane < HEAD_DIM, kn, 0.0).astype(BF16).astype(F32)
        return kb, _value_tile_t(x.T[HEAD_DIM:, :]), _running_max(kmax_scr, idx, _group_norms(kb, LANES)[0])

    ks, vst_ref[0, 0], ks_max = key_pair(kv_ref[0, :, LANES:2 * LANES], kg_ref[0:1, :], 0)
    kw, vwt_ref[0, 0], kw_max = key_pair(kv_ref[0, :, 2 * LANES:3 * LANES], kg_ref[1:2, :], 1)
    ks_ref[0] = ks.astype(BF16)
    kw_ref[0] = jnp.where(lane == HEAD_DIM, 1.0, kw).astype(BF16)

    q2_t = (qr_t * SCALE_LOG2).astype(BF16).astype(F32)
    row_h = _iota((HEAD_DIM, T), 0)
    for h, q_norm in enumerate(_row_group_norms(q2_t, HEAD_DIM)):
        qt_ref[0, h, 0:LANES, :] = (bias - q_norm * (ks_max * BOUND_SLACK)).astype(BF16)
        shift_rows = jnp.where(row_h == 0, -q_norm * (kw_max * BOUND_SLACK), 0.0)
        qt_ref[0, h, LANES:AUG_W, :] = jnp.concatenate([q2_t[h * HEAD_DIM:(h + 1) * HEAD_DIM], shift_rows],
                                                       axis=0).astype(BF16)


def _nsa_prep(p3, tables, qg_t, kg2, kc, vc_t, overlap_t, T=256):
    B, S, _ = p3.shape
    T = min(T, S)
    nc = kc.shape[1]
    n_sel = S // SEL_BLOCK
    tab = pl.BlockSpec((T, ATT_WIDTH), lambda b, i: (i, 0))
    tab_t = pl.BlockSpec((ATT_WIDTH, T), lambda b, i: (0, i))
    v_t = pl.BlockSpec((1, 1, VT_ROWS, T), lambda b, i: (b, i, 0, 0))
    v_t_shape = jax.ShapeDtypeStruct((B, S // T, VT_ROWS, T), BF16)
    return pl.pallas_call(
        functools.partial(_nsa_prep_kernel, n_sel=n_sel),
        grid=(B, S // T),
        in_specs=[pl.BlockSpec((1, T, ATT_WIDTH), lambda b, i: (b, i, COL_NQ // ATT_WIDTH)),
                  pl.BlockSpec((1, T, 3 * LANES), lambda b, i: (b, i, COL_NKV // (3 * LANES))),
                  tab, tab, tab_t, tab_t,
                  _const_spec((ATT_WIDTH, T)), _const_spec((2, LANES)),
                  pl.BlockSpec((1, nc, LANES), lambda b, i: (b, 0, 0)),
                  pl.BlockSpec((1, HEAD_DIM, nc), lambda b, i: (b, 0, 0)),
                  _const_spec((LANES, nc))],
        out_specs=[pl.BlockSpec((1, T, ATT_WIDTH), lambda b, i: (b, i, 0)),
                   pl.BlockSpec((1, ATT_HEADS, AUG_W, T), lambda b, i: (b, 0, 0, i)),
                   pl.BlockSpec((1, T, LANES), lambda b, i: (b, i, 0)),
                   pl.BlockSpec((1, T, LANES), lambda b, i: (b, i, 0)),
                   v_t, v_t],
        out_shape=[jax.ShapeDtypeStruct((B, S, ATT_WIDTH), F32),
                   jax.ShapeDtypeStruct((B, ATT_HEADS, AUG_W, S), BF16),
                   jax.ShapeDtypeStruct((B, S, LANES), BF16),
                   jax.ShapeDtypeStruct((B, S, LANES), BF16),
                   v_t_shape, v_t_shape],
        scratch_shapes=[pltpu.VMEM((2, SUBLANES, LANES), F32)],
        compiler_params=_cparams(2),
        name="nsa_prep",
    )(p3, p3, *tables, qg_t, kg2, kc, vc_t, overlap_t)


def _out_ffn_kernel(x_ref, ym_ref, yb_ref, yn_ref, wo_ref, g_ref, w1_ref, w2_ref, o_ref, *, fc):
    mix = jnp.concatenate([ym_ref[...], yb_ref[...], yn_ref[...]], axis=1)
    x = x_ref[...] + jnp.dot(mix, wo_ref[...], preferred_element_type=F32)
    hb = (x * lax.rsqrt(jnp.mean(x * x, axis=-1, keepdims=True) + NORM_EPS) * g_ref[...]).astype(BF16)
    acc = x
    for c in range(w1_ref.shape[1] // fc):
        u = jnp.maximum(jnp.dot(hb, w1_ref[:, c * fc:(c + 1) * fc], preferred_element_type=F32), 0.0)
        acc = acc + jnp.dot((u * u).astype(BF16), w2_ref[c * fc:(c + 1) * fc, :], preferred_element_type=F32)
    o_ref[...] = acc


def _out_ffn(x2d, y_m, y_b, y_n, wo, g, w1, w2, tm=512, fc=1024):
    M, D = x2d.shape
    tm = min(tm, M)
    rows = lambda w: pl.BlockSpec((tm, w), lambda i: (i, 0))
    resident = lambda a: pl.BlockSpec(a.shape, lambda i: (0,) * a.ndim, pipeline_mode=pl.Buffered(1))
    return pl.pallas_call(
        functools.partial(_out_ffn_kernel, fc=fc),
        grid=(M // tm,),
        in_specs=[rows(D), rows(M_WIDTH), rows(ATT_WIDTH), rows(ATT_WIDTH),
                  resident(wo), _const_spec((1, D)), resident(w1), resident(w2)],
        out_specs=rows(D),
        out_shape=jax.ShapeDtypeStruct((M, D), F32),
        compiler_params=_cparams(1),
        name="out_ffn",
    )(x2d, y_m, y_b, y_n, wo, g, w1, w2)


def _rope_tables(S):
    inv_freq = jnp.exp(-math.log(ROPE_THETA) * jnp.arange(ROT_HALF, dtype=F32) * (2.0 / ROT_DIM))
    ang = jnp.arange(S, dtype=F32)[:, None] * inv_freq[None, :]
    cos, sin = jnp.cos(ang), jnp.sin(ang)
    rest = HEAD_DIM - ROT_DIM
    cos64 = jnp.concatenate([cos, cos, jnp.ones((S, rest), F32)], axis=1)
    sin64 = jnp.concatenate([-sin, sin, jnp.zeros((S, rest), F32)], axis=1)
    return jnp.tile(cos64, (1, ATT_HEADS)), jnp.tile(sin64, (1, ATT_HEADS))


def _overlap_matrix(nc):
    c_start = np.arange(nc)[:, None] * CMP_STRIDE
    s_start = np.arange(LANES)[None, :] * SEL_BLOCK
    return jnp.asarray(((c_start < s_start + SEL_BLOCK) & (c_start + CMP_LEN > s_start)).astype(np.float32))


def _compress_weights(pe, w1, w2):
    half = CMP_LEN // 2
    pe_r = jnp.concatenate([pe[0], pe[1]], axis=-1)
    pea = pe_r[:half].reshape(1, half * LANES)
    peb = pe_r[half:].reshape(1, half * LANES)
    w1r = w1.reshape(2, CMP_LEN, HEAD_DIM, CMP_HIDDEN)
    z = jnp.zeros_like(w1r[0])
    wk = jnp.concatenate([w1r[0], z], axis=-1)
    wv = jnp.concatenate([z, w1r[1]], axis=-1)
    wboth = jnp.concatenate([wk, wv], axis=1)
    w1a = wboth[:half].reshape(half * LANES, 2 * CMP_HIDDEN).astype(BF16)
    w1b = wboth[half:].reshape(half * LANES, 2 * CMP_HIDDEN).astype(BF16)
    z2 = jnp.zeros_like(w2[0])
    w2bd = jnp.concatenate([jnp.concatenate([w2[0], z2], axis=1),
                            jnp.concatenate([z2, w2[1]], axis=1)], axis=0).astype(BF16)
    return pea, peb, w1a, w1b, w2bd


def _pad_lanes(v, width=LANES):
    return jnp.concatenate([v, jnp.zeros((width - v.shape[0],), v.dtype)])[None, :]


def kernel(x, w_in, b_if, conv_qk, m_norm, moba_qk_norm, nsa_q_norm, nsa_k_norm, cmp_pe, cmp_w1, cmp_w2,
           w_out, norm_mix, norm_ffn, w_ff1, w_ff2):
    B, S, D = x.shape
    depth = w_in.shape[0]
    cos, sin = _rope_tables(S)
    tables = (cos, sin, cos.T, sin.T)
    overlap_t = _overlap_matrix(S // CMP_STRIDE).T
    x2d = x.reshape(B * S, D)
    for l in range(depth):
        p3 = _proj(x2d, norm_mix[l][None, :], w_in, l).reshape(B, S, P_W)

        y_m = _mlstm(p3, conv_qk[l], _pad_lanes(b_if[l]), m_norm[l][None, :])

        tile_g = lambda g: jnp.tile(g, ATT_HEADS)[None, :]
        rows_g = lambda g: jnp.broadcast_to(jnp.tile(g, ATT_HEADS)[:, None], (ATT_WIDTH, min(MOBA_BLOCK, S)))
        qt_b, kp_b, vt_b = _moba_prep(p3, tables, rows_g(moba_qk_norm[l, 0]), tile_g(moba_qk_norm[l, 1]))
        y_b = _flash(qt_b, kp_b, vt_b, BF16, MOBA_BLOCK)

        pea, peb, w1a, w1b, w2bd = _compress_weights(cmp_pe[l], cmp_w1[l], cmp_w2[l])
        kc, vc_t = _compress(p3, pea, peb, w1a, w1b, w2bd, _pad_lanes(nsa_k_norm[l, 0]))
        kg2 = jnp.concatenate([_pad_lanes(nsa_k_norm[l, 1]), _pad_lanes(nsa_k_norm[l, 2])], axis=0)
        o_c, qt_n, ks, kw, vs_t, vw_t = _nsa_prep(p3, tables, rows_g(nsa_q_norm[l]), kg2, kc, vc_t, overlap_t)
        o_s = _flash(qt_n, ks[:, None], vs_t[:, None], F32, SEL_BLOCK)
        y_n = _window_combine(qt_n, kw, vw_t, o_c, o_s, p3)

        x2d = _out_ffn(x2d, y_m.reshape(B * S, M_WIDTH), y_b.reshape(B * S, ATT_WIDTH), y_n.reshape(B * S, ATT_WIDTH),
                       w_out[l].astype(BF16), norm_ffn[l][None, :], w_ff1[l].astype(BF16), w_ff2[l].astype(BF16))
    return x2d.reshape(B, S, D)
```

```python
import functools
import math

import jax
import jax.numpy as jnp
import numpy as np
from jax import lax
from jax.experimental import pallas as pl
from jax.experimental.pallas import tpu as pltpu

F32 = jnp.float32
BF16 = jnp.bfloat16

HEAD_DIM = 64
M_HEADS = 4
M_HEAD_DIM = 128
M_WIDTH = M_HEADS * M_HEAD_DIM
CONV_W = 4
ATT_HEADS = 4
ATT_WIDTH = ATT_HEADS * HEAD_DIM
MOBA_BLOCK = 256
MOBA_TOPK = 3
CMP_LEN = 32
CMP_STRIDE = 16
CMP_HIDDEN = 128
SEL_BLOCK = 64
SEL_TOPK = 16
WINDOW = 512
ROPE_THETA = 500000.0
ROT_DIM = HEAD_DIM // 4
ROT_HALF = ROT_DIM // 2
NORM_EPS = 1e-6
NEG = -1e30
BIG = 1e9
SCALE = HEAD_DIM ** -0.5
SCALE_LOG2 = SCALE * math.log2(math.e)
BOUND_SLACK = 1.0 + 2.0 ** -7
L_TINY = 2.0 ** -100

SUBLANES = 8
LANES = 128
AUG_W = 2 * LANES
VT_ROWS = HEAD_DIM + 16
ATTEND_GROUP = 4
VMEM_LIMIT = 56 * 1024 * 1024

COL_MQK, COL_MV, COL_MO = 0, 1024, 1536
COL_BQ, COL_BK, COL_BV = 2048, 2304, 2560
COL_NQ, COL_NKV, COL_SM = 2816, 3072, 3456
P_W = 3584
SM_I, SM_F, SM_G = 0, 4, 8
_W_ATT0 = 4 * M_WIDTH + 2 * M_HEADS
_W_ATT1 = _W_ATT0 + 4 * ATT_WIDTH + 6 * HEAD_DIM
W_IN_SECTIONS = ((0, 4 * M_WIDTH), (_W_ATT0, _W_ATT1 - _W_ATT0), (4 * M_WIDTH, 2 * M_HEADS), (_W_ATT1, 3 * ATT_HEADS))

_NT = (((1,), (1,)), ((), ()))
_HI = lax.Precision.HIGHEST


def _iota(shape, dim):
    return lax.broadcasted_iota(jnp.int32, shape, dim)


def _cparams(n_axes):
    return pltpu.CompilerParams(dimension_semantics=("arbitrary",) * n_axes,
                                vmem_limit_bytes=VMEM_LIMIT)


def _const_spec(shape):
    nd = len(shape)
    return pl.BlockSpec(shape, lambda *_: (0,) * nd)


def _rms_groups(x, width):
    T, W = x.shape
    x2 = x * x
    lane = _iota((T, W), 1)
    scale = None
    for h in range(W // width):
        r = lax.rsqrt(jnp.mean(x2[:, h * width:(h + 1) * width], axis=-1, keepdims=True) + NORM_EPS)
        scale = r if scale is None else jnp.where(lane >= h * width, r, scale)
    return x * scale


def _rope(x, cos, sin):
    W = x.shape[1]
    lane = _iota(x.shape, 1) % HEAD_DIM
    up = pltpu.roll(x, W - ROT_HALF, 1)
    dn = pltpu.roll(x, ROT_HALF, 1)
    return x * cos + jnp.where(lane < ROT_HALF, up, dn) * sin


def _group_norms(x, width):
    x2 = x * x
    return [jnp.sqrt(jnp.sum(x2[:, h * width:(h + 1) * width], axis=-1, keepdims=True))
            for h in range(x.shape[1] // width)]


def _running_max(scr, idx, col):
    new = jnp.maximum(scr[idx][0:1, 0:1], jnp.max(col, axis=0, keepdims=True))
    scr[idx] = jnp.broadcast_to(new, scr.shape[1:])
    return new


def _value_tile_t(v_t):
    extra = jnp.where(_iota((VT_ROWS - HEAD_DIM, v_t.shape[1]), 0) == 0, 1.0, 0.0)
    return jnp.concatenate([v_t, extra], axis=0).astype(BF16)


def _rms_row_groups(x_t, width):
    parts = []
    for h in range(x_t.shape[0] // width):
        g = x_t[h * width:(h + 1) * width]
        parts.append(g * lax.rsqrt(jnp.mean(g * g, axis=0, keepdims=True) + NORM_EPS))
    return jnp.concatenate(parts, axis=0)


def _rope_t(x_t, cos_t, sin_t):
    W = x_t.shape[0]
    row = _iota(x_t.shape, 0) % HEAD_DIM
    up = pltpu.roll(x_t, W - ROT_HALF, 0)
    dn = pltpu.roll(x_t, ROT_HALF, 0)
    return x_t * cos_t + jnp.where(row < ROT_HALF, up, dn) * sin_t


def _row_group_norms(x_t, width):
    return [jnp.sqrt(jnp.sum(jnp.square(x_t[h * width:(h + 1) * width]), axis=0, keepdims=True))
            for h in range(x_t.shape[0] // width)]


def _top_k_mask_t(vals, k, rank_limit=None):
    row = _iota(vals.shape, 0)
    sel = jnp.zeros(vals.shape, jnp.int32)
    for r in range(k):
        mx = jnp.max(vals, axis=0, keepdims=True)
        idx = jnp.min(jnp.where(vals == mx, row, LANES), axis=0, keepdims=True)
        pick = row == idx
        mark = 1 if rank_limit is None else jnp.where(rank_limit > r, 1, 0)
        sel = jnp.where(pick, mark, sel)
        vals = jnp.where(pick, -jnp.inf, vals)
    return sel


def _proj_kernel(x_ref, g_ref, w_ref, o_ref, w_scr):
    @pl.when(pl.program_id(0) == 0)
    def _():
        dst = 0
        for src, width in W_IN_SECTIONS:
            w_scr[:, dst:dst + width] = w_ref[0, :, src:src + width].astype(BF16)
            dst += width
        w_scr[:, dst:] = jnp.zeros((w_scr.shape[0], w_scr.shape[1] - dst), BF16)

    x = x_ref[...]
    h = x * lax.rsqrt(jnp.mean(x * x, axis=-1, keepdims=True) + NORM_EPS) * g_ref[...]
    o_ref[...] = jnp.dot(h.astype(BF16), w_scr[...], preferred_element_type=F32)


def _proj(x2d, g, w_all, layer, tm=512):
    M, D = x2d.shape
    N = P_W
    assert w_all.shape[2] == sum(width for _, width in W_IN_SECTIONS)
    return pl.pallas_call(
        _proj_kernel,
        grid=(M // tm,),
        in_specs=[pl.BlockSpec((tm, D), lambda i: (i, 0)), _const_spec((1, D)),
                  pl.BlockSpec((1, D, w_all.shape[2]), lambda i: (layer, 0, 0), pipeline_mode=pl.Buffered(1))],
        out_specs=pl.BlockSpec((tm, N), lambda i: (i, 0)),
        out_shape=jax.ShapeDtypeStruct((M, N), F32),
        scratch_shapes=[pltpu.VMEM((D, N), BF16)],
        compiler_params=_cparams(1),
        name="in_proj",
    )(x2d, g, w_all)


def _mlstm_kernel(qk_ref, v_ref, o_ref, sm_ref, cw_ref, b_ref, g_ref, out_ref, xbuf, c_scr, m_scr, *, Lc):
    W2 = 2 * M_WIDTH
    D = M_HEAD_DIM

    @pl.when(pl.program_id(1) == 0)
    def _():
        xbuf[0:SUBLANES, :] = jnp.zeros((SUBLANES, W2), F32)
        c_scr[...] = jnp.zeros_like(c_scr)
        m_scr[...] = jnp.zeros_like(m_scr)

    x = qk_ref[0]
    xbuf[SUBLANES:SUBLANES + Lc, :] = x
    first = SUBLANES - CONV_W + 1
    y = cw_ref[0:1, :] * xbuf[first:first + Lc, :]
    for j in range(1, CONV_W):
        y = y + cw_ref[j:j + 1, :] * xbuf[first + j:first + j + Lc, :]
    xbuf[0:SUBLANES, :] = x[Lc - SUBLANES:Lc, :]
    qk = y * jax.nn.sigmoid(y)

    gi = sm_ref[0] + b_ref[...]
    lane = _iota((Lc, LANES), 1)
    lsig = jnp.minimum(gi, 0.0) - jnp.log1p(jnp.exp(-jnp.abs(gi)))
    gates = jnp.where(lane < SM_F, gi, jnp.where(lane < SM_F + M_HEADS, lsig, 0.0))
    gates_t = gates.T
    ri, ci = _iota((Lc, Lc), 0), _iota((Lc, Lc), 1)
    causal = ci <= ri
    b_col = jnp.dot(causal.astype(F32), gates, precision=_HI, preferred_element_type=F32)
    b_row = jnp.dot(gates_t[0:2 * M_HEADS, :], (ri <= ci).astype(F32), precision=_HI,
                    preferred_element_type=F32)
    ones_col = jnp.where(lane == 0, 1.0, 0.0)

    for h in range(M_HEADS):
        q = qk[:, h * D:(h + 1) * D]
        k = qk[:, M_WIDTH + h * D:M_WIDTH + (h + 1) * D] * (D ** -0.5)
        v_aug = jnp.concatenate([v_ref[0, :, h * D:(h + 1) * D], ones_col], axis=1).astype(BF16)
        bc = b_col[:, SM_F + h:SM_F + h + 1]
        br = b_row[SM_F + h:SM_F + h + 1, :]
        li_r = gates_t[SM_I + h:SM_I + h + 1, :]
        li_c = gates[:, SM_I + h:SM_I + h + 1]
        m_prev = m_scr[h][0:1, 0:1]

        dmat = jnp.where(causal, bc - br + li_r, -jnp.inf)
        inter = bc + m_prev
        m_t = jnp.maximum(inter, jnp.max(dmat, axis=-1, keepdims=True))
        w_intra = jnp.exp(dmat - m_t)
        w_prev = jnp.exp(inter - m_t)
        qb = q.astype(BF16)
        s = lax.dot_general(qb, k.astype(BF16), _NT, preferred_element_type=F32) * w_intra
        tot = (jnp.dot(s.astype(BF16), v_aug, preferred_element_type=F32)
               + w_prev * jnp.dot(qb, c_scr[h].astype(BF16), preferred_element_type=F32))
        den = tot[:, D:D + 1]
        hh = tot[:, :D] / jnp.maximum(jnp.abs(den), jnp.exp(-m_t))

        b_last = bc[Lc - 1:Lc, :]
        g_c = b_last - bc + li_c
        m_new = jnp.maximum(b_last + m_prev, jnp.max(g_c, axis=0, keepdims=True))
        a = jnp.exp(b_last + m_prev - m_new)
        kw_t = (k * jnp.exp(g_c - m_new)).T.astype(BF16)
        c_scr[h] = a * c_scr[h] + jnp.dot(kw_t, v_aug, preferred_element_type=F32)
        m_scr[h] = jnp.broadcast_to(m_new, (SUBLANES, LANES))

        hn = hh * lax.rsqrt(jnp.mean(hh * hh, axis=-1, keepdims=True) + NORM_EPS) * g_ref[0:1, h * D:(h + 1) * D]
        out_ref[0, :, h * D:(h + 1) * D] = (jax.nn.sigmoid(o_ref[0, :, h * D:(h + 1) * D]) * hn).astype(BF16)


def _mlstm(p3, conv_w, b_sm, m_norm, Lc=256):
    B, S, _ = p3.shape
    Lc = min(Lc, S)
    kern = functools.partial(_mlstm_kernel, Lc=Lc)
    return pl.pallas_call(
        kern,
        grid=(B, S // Lc),
        in_specs=[
            pl.BlockSpec((1, Lc, 2 * M_WIDTH), lambda b, c: (b, c, COL_MQK // (2 * M_WIDTH))),
            pl.BlockSpec((1, Lc, M_WIDTH), lambda b, c: (b, c, COL_MV // M_WIDTH)),
            pl.BlockSpec((1, Lc, M_WIDTH), lambda b, c: (b, c, COL_MO // M_WIDTH)),
            pl.BlockSpec((1, Lc, LANES), lambda b, c: (b, c, COL_SM // LANES)),
            _const_spec((CONV_W, 2 * M_WIDTH)),
            _const_spec((1, LANES)),
            _const_spec((1, M_WIDTH)),
        ],
        out_specs=pl.BlockSpec((1, Lc, M_WIDTH), lambda b, c: (b, c, 0)),
        out_shape=jax.ShapeDtypeStruct((B, S, M_WIDTH), BF16),
        scratch_shapes=[
            pltpu.VMEM((Lc + SUBLANES, 2 * M_WIDTH), F32),
            pltpu.VMEM((M_HEADS, M_HEAD_DIM, 2 * M_HEAD_DIM), F32),
            pltpu.VMEM((M_HEADS, SUBLANES, LANES), F32),
        ],
        compiler_params=_cparams(2),
        name="mlstm",
    )(p3, p3, p3, p3, conv_w, b_sm, m_norm)


def _moba_prep_kernel(q_ref, k_ref, v_ref, cos_ref, sin_ref, cost_ref, sint_ref, qg_ref, kg_ref,
                      qt_ref, kp_ref, vt_ref, kmean_scr, kmax_scr):
    i = pl.program_id(1)
    T = q_ref.shape[1]

    @pl.when(i == 0)
    def _():
        kmean_scr[...] = jnp.zeros_like(kmean_scr)
        kmax_scr[...] = jnp.zeros_like(kmax_scr)

    kn = _rope(_rms_groups(k_ref[0], HEAD_DIM) * kg_ref[...], cos_ref[...], sin_ref[...])
    kb = kn.astype(BF16).astype(F32)
    k_norms = _group_norms(kb, HEAD_DIM)
    qn_t = _rope_t(_rms_row_groups(q_ref[0].T, HEAD_DIM) * qg_ref[...], cost_ref[...], sint_ref[...])
    q2_t = (qn_t * SCALE_LOG2).astype(BF16).astype(F32)
    q_norms = _row_group_norms(q2_t, HEAD_DIM)
    v_t = v_ref[0].T
    row_w = _iota((ATT_WIDTH, T), 0)
    row = _iota((LANES, T), 0)
    zeros_half = jnp.zeros((HEAD_DIM, T), F32)
    kmeans = kmean_scr[...]
    for h in range(ATT_HEADS):
        qm = jnp.where((row_w >= h * HEAD_DIM) & (row_w < (h + 1) * HEAD_DIM), qn_t, 0.0)
        gs = jnp.dot(kmeans, qm, precision=_HI, preferred_element_type=F32)
        sel = _top_k_mask_t(jnp.where(row < i, gs, NEG), MOBA_TOPK, rank_limit=i)
        bound = q_norms[h] * (_running_max(kmax_scr, h, k_norms[h]) * BOUND_SLACK)
        qt_ref[0, h, 0:LANES, :] = (jnp.where((sel > 0) | (row == i), 0.0, NEG) - bound).astype(BF16)
        q_h = q2_t[h * HEAD_DIM:(h + 1) * HEAD_DIM]
        qt_ref[0, h, LANES:AUG_W, :] = jnp.concatenate([zeros_half, q_h] if h % 2 else [q_h, zeros_half],
                                                       axis=0).astype(BF16)
        vt_ref[0, h, 0] = _value_tile_t(v_t[h * HEAD_DIM:(h + 1) * HEAD_DIM, :])
    for j in range(ATT_HEADS // 2):
        kp_ref[0, j] = kb[:, j * LANES:(j + 1) * LANES].astype(BF16)
    kmean_scr[pl.ds(i, 1), :] = jnp.mean(kn, axis=0, keepdims=True)


def _moba_prep(p3, tables, qg_t, kg):
    B, S, _ = p3.shape
    T = MOBA_BLOCK
    blk = lambda col: pl.BlockSpec((1, T, ATT_WIDTH), lambda b, i: (b, i, col // ATT_WIDTH))
    tab = pl.BlockSpec((T, ATT_WIDTH), lambda b, i: (i, 0))
    tab_t = pl.BlockSpec((ATT_WIDTH, T), lambda b, i: (0, i))
    return pl.pallas_call(
        _moba_prep_kernel,
        grid=(B, S // T),
        in_specs=[blk(COL_BQ), blk(COL_BK), blk(COL_BV), tab, tab, tab_t, tab_t,
                  _const_spec((ATT_WIDTH, T)), _const_spec((1, ATT_WIDTH))],
        out_specs=[pl.BlockSpec((1, ATT_HEADS, AUG_W, T), lambda b, i: (b, 0, 0, i)),
                   pl.BlockSpec((1, ATT_HEADS // 2, T, LANES), lambda b, i: (b, 0, i, 0)),
                   pl.BlockSpec((1, ATT_HEADS, 1, VT_ROWS, T), lambda b, i: (b, 0, i, 0, 0))],
        out_shape=[jax.ShapeDtypeStruct((B, ATT_HEADS, AUG_W, S), BF16),
                   jax.ShapeDtypeStruct((B, ATT_HEADS // 2, S, LANES), BF16),
                   jax.ShapeDtypeStruct((B, ATT_HEADS, S // T, VT_ROWS, T), BF16)],
        scratch_shapes=[pltpu.VMEM((LANES, ATT_WIDTH), F32), pltpu.VMEM((ATT_HEADS, SUBLANES, LANES), F32)],
        compiler_params=_cparams(2),
        name="moba_prep",
    )(p3, p3, p3, *tables, qg_t, kg)


def _attend(tile_scores, tile_pv, lo, hi, acc_scr, m_scr, own_width=1):
    n_chain = acc_scr.shape[0]

    def run(exact):
        acc_scr[...] = jnp.zeros_like(acc_scr)
        if exact:
            m_scr[...] = jnp.full(m_scr.shape, NEG, F32)

        def tile(j, width, last):
            for c in range(n_chain):
                s_t = tile_scores(c, j, width, last)
                if exact:
                    m_prev = m_scr[c]
                    m_new = jnp.maximum(m_prev, jnp.max(s_t, axis=0, keepdims=True))
                    acc_scr[c] = (acc_scr[c] * jnp.exp2(m_prev - m_new)
                                  + tile_pv(c, j, width, jnp.exp2(s_t - m_new).astype(BF16)))
                    m_scr[c] = m_new
                else:
                    acc_scr[c] += tile_pv(c, j, width, jnp.exp2(s_t).astype(BF16))

        group = 1 if exact else ATTEND_GROUP
        n = hi - lo

        def body(i, carry):
            tile(lo + group * i, group, False)
            return carry

        lax.fori_loop(0, n // group, body, 0)
        start = lo + (n // group) * group
        merged = own_width if own_width < group else 0
        b = group // 2
        while b:
            if b != merged:
                take = (n & b) != 0

                @pl.when(take)
                def _(start=start, b=b):
                    tile(start, b, False)

                start = start + jnp.where(take, b, 0)
            b //= 2

        if merged:
            take = (n & merged) != 0

            @pl.when(take)
            def _():
                tile(hi - merged, merged + own_width, True)

            @pl.when(jnp.logical_not(take))
            def _():
                tile(hi, own_width, True)
        else:
            tile(hi, own_width, True)

    run(False)
    l_min = jnp.min(acc_scr[:, HEAD_DIM:HEAD_DIM + 1, :])

    @pl.when(jnp.logical_not(l_min >= L_TINY))
    def _():
        run(True)


def _head_pair_rows(o_a, o_b):
    return jnp.concatenate([o_a, o_b], axis=0).T


def _normalised_heads(acc_scr, T):
    heads = []
    for c in range(acc_scr.shape[0]):
        o = acc_scr[c, 0:HEAD_DIM, :] / acc_scr[c, HEAD_DIM:HEAD_DIM + 1, :]
        heads += [o[:, h * T:(h + 1) * T] for h in range(o.shape[1] // T)]
    return heads


def _flash_kernel(qt_ref, k_ref, vt_ref, o_ref, acc_scr, m_scr, *, TQ, mask_block):
    qi = pl.program_id(1)
    P, G, VH = qt_ref.shape[1], k_ref.shape[1], vt_ref.shape[1]
    TK = vt_ref.shape[4]
    own = TQ // TK
    hc = P // G
    N = hc * TQ
    q_cat = [jnp.concatenate([qt_ref[0, c * hc + h] for h in range(hc)], axis=1) for c in range(G)]
    def tile_scores(c, j, width, last):
        rows = width * TK
        lane, row = _iota((rows, LANES), 1), _iota((rows, LANES), 0)
        onehot = jnp.where(lane == (j * TK + row) // mask_block, 1.0, 0.0).astype(BF16)
        kt = jnp.concatenate([onehot, k_ref[0, c, pl.ds(pl.multiple_of(j * TK, TK), rows), :]], axis=1)
        s_t = jnp.dot(kt, q_cat[c], preferred_element_type=F32)
        if last:
            key = _iota((rows, N), 0) - (rows - TQ)
            s_t = jnp.where(key <= _iota((rows, N), 1) % TQ, s_t, NEG)
        return s_t

    def values_t(v, j, width):
        return jnp.concatenate([vt_ref[0, v, j + u] for u in range(width)], axis=1)

    def tile_pv(c, j, width, p):
        if VH == G:
            return jnp.dot(values_t(c, j, width), p, preferred_element_type=F32)
        return jnp.concatenate([jnp.dot(values_t(c * hc + h, j, width), p[:, h * TQ:(h + 1) * TQ],
                                        preferred_element_type=F32) for h in range(hc)], axis=1)

    _attend(tile_scores, tile_pv, 0, qi * own, acc_scr, m_scr, own_width=own)
    heads = _normalised_heads(acc_scr, TQ)
    for j in range(P // 2):
        o_ref[0, :, j * LANES:(j + 1) * LANES] = _head_pair_rows(heads[2 * j], heads[2 * j + 1]).astype(o_ref.dtype)


def _flash(q_t, k, v_t, out_dtype, mask_block, TQ=512):
    B, P, _, S = q_t.shape
    KH, VH, TK = k.shape[1], v_t.shape[1], v_t.shape[4]
    TQ = max(min(TQ, S), TK)
    assert v_t.shape == (B, VH, S // TK, VT_ROWS, TK) and k.shape == (B, KH, S, LANES) and TQ % TK == 0
    return pl.pallas_call(
        functools.partial(_flash_kernel, TQ=TQ, mask_block=mask_block),
        grid=(B, S // TQ),
        in_specs=[pl.BlockSpec((1, P, AUG_W, TQ), lambda b, i: (b, 0, 0, i)),
                  pl.BlockSpec((1, KH, S, LANES), lambda b, i: (b, 0, 0, 0)),
                  pl.BlockSpec((1, VH, S // TK, VT_ROWS, TK), lambda b, i: (b, 0, 0, 0, 0))],
        out_specs=pl.BlockSpec((1, TQ, P * HEAD_DIM), lambda b, i: (b, i, 0)),
        out_shape=jax.ShapeDtypeStruct((B, S, P * HEAD_DIM), out_dtype),
        scratch_shapes=[pltpu.VMEM((KH, VT_ROWS, P // KH * TQ), F32), pltpu.VMEM((KH, 1, P // KH * TQ), F32)],
        compiler_params=_cparams(2),
        name="flash",
    )(q_t, k, v_t)


def _window_kernel(qt_ref, k_ref, vt_ref, oc_ref, os_ref, sm_ref, y_ref, acc_scr, m_scr, *, T):
    R = ATT_HEADS
    TK = vt_ref.shape[3]
    own = T // TK
    qi = pl.program_id(1)
    q_cat = jnp.concatenate([qt_ref[0, h, LANES:AUG_W, :] for h in range(R)], axis=1)

    def tile_scores(c, j, width, last):
        rows = width * TK
        kt = k_ref[0, pl.ds(pl.multiple_of(j * TK, TK), rows), :]
        s_t = jnp.dot(kt, q_cat, preferred_element_type=F32)
        tq = qi * T + _iota((rows, R * T), 1) % T
        key = j * TK + _iota((rows, R * T), 0)
        return jnp.where((key <= tq) & (key > tq - WINDOW), s_t, NEG)

    def tile_pv(c, j, width, p):
        v_t = jnp.concatenate([vt_ref[0, j + u] for u in range(width)], axis=1)
        return jnp.dot(v_t, p, preferred_element_type=F32)

    first = jnp.maximum(qi * own - (WINDOW + TK - 1) // TK, 0)
    _attend(tile_scores, tile_pv, first, qi * own, acc_scr, m_scr, own_width=own)
    sig = jax.nn.sigmoid(sm_ref[0])
    lane = _iota((T, LANES), 1)
    heads = _normalised_heads(acc_scr, T)
    for j in range(R // 2):
        o_w = _head_pair_rows(heads[2 * j], heads[2 * j + 1])

        def gate(branch):
            c = SM_G + branch * R + 2 * j
            return jnp.where(lane < HEAD_DIM, sig[:, c:c + 1], sig[:, c + 1:c + 2])
        sl = slice(j * LANES, (j + 1) * LANES)
        y_ref[0, :, sl] = (gate(0) * oc_ref[0, :, sl] + gate(1) * os_ref[0, :, sl] + gate(2) * o_w).astype(y_ref.dtype)


def _window_combine(q_t, kw, vw_t, o_c, o_s, p3, T=512):
    B, R, _, S = q_t.shape
    TK = vw_t.shape[3]
    T = max(min(T, S), TK)
    packed = pl.BlockSpec((1, T, ATT_WIDTH), lambda b, i: (b, i, 0))
    return pl.pallas_call(
        functools.partial(_window_kernel, T=T),
        grid=(B, S // T),
        in_specs=[pl.BlockSpec((1, R, AUG_W, T), lambda b, i: (b, 0, 0, i)),
                  pl.BlockSpec((1, S, LANES), lambda b, i: (b, 0, 0)),
                  pl.BlockSpec((1, S // TK, VT_ROWS, TK), lambda b, i: (b, 0, 0, 0)),
                  packed, packed,
                  pl.BlockSpec((1, T, LANES), lambda b, i: (b, i, COL_SM // LANES))],
        out_specs=packed,
        out_shape=jax.ShapeDtypeStruct((B, S, ATT_WIDTH), BF16),
        scratch_shapes=[pltpu.VMEM((1, VT_ROWS, R * T), F32), pltpu.VMEM((1, 1, R * T), F32)],
        compiler_params=_cparams(2),
        name="nsa_window",
    )(q_t, kw, vw_t, o_c, o_s, p3)


def _compress_kernel(t_ref, pea_ref, peb_ref, w1a_ref, w1b_ref, w2_ref, kg_ref, kc_ref, vct_ref, t_scr):
    n = t_scr.shape[0]
    for r in range(CMP_STRIDE):
        t_scr[:, r * LANES:(r + 1) * LANES] = t_ref[0, pl.ds(r, n, stride=CMP_STRIDE), :]
    t = t_scr[...]
    a = jnp.dot((t + pea_ref[...]).astype(BF16), w1a_ref[...], preferred_element_type=F32)
    b = jnp.dot((t + peb_ref[...]).astype(BF16), w1b_ref[...], preferred_element_type=F32)
    hid = a + pltpu.roll(b, n - 1, 0)
    hid = hid * jax.nn.sigmoid(hid)
    kv = jnp.dot(hid.astype(BF16), w2_ref[...], preferred_element_type=F32)
    lane = _iota(kv.shape, 1)
    ms = jnp.sum(jnp.where(lane < HEAD_DIM, kv * kv, 0.0), axis=-1, keepdims=True) * (1.0 / HEAD_DIM)
    kc_ref[0] = jnp.where(lane < HEAD_DIM, kv * lax.rsqrt(ms + NORM_EPS) * kg_ref[...], 0.0).astype(kc_ref.dtype)
    vct_ref[0] = kv.T[HEAD_DIM:, :].astype(vct_ref.dtype)


def _compress(p3, pea, peb, w1a, w1b, w2, kg):
    B, S, _ = p3.shape
    n, W = S // CMP_STRIDE, CMP_STRIDE * LANES
    return pl.pallas_call(
        _compress_kernel,
        grid=(B,),
        in_specs=[pl.BlockSpec((1, S, LANES), lambda b: (b, 0, COL_NKV // LANES)),
                  _const_spec((1, W)), _const_spec((1, W)),
                  _const_spec(w1a.shape), _const_spec(w1b.shape), _const_spec(w2.shape),
                  _const_spec((1, LANES))],
        out_specs=[pl.BlockSpec((1, n, LANES), lambda b: (b, 0, 0)),
                   pl.BlockSpec((1, HEAD_DIM, n), lambda b: (b, 0, 0))],
        out_shape=[jax.ShapeDtypeStruct((B, n, LANES), BF16), jax.ShapeDtypeStruct((B, HEAD_DIM, n), BF16)],
        scratch_shapes=[pltpu.VMEM((n, W), F32)],
        compiler_params=_cparams(1),
        name="nsa_compress",
    )(p3, pea, peb, w1a, w1b, w2, kg)


def _nsa_prep_kernel(q_ref, kv_ref, cos_ref, sin_ref, cost_ref, sint_ref, qg_ref, kg_ref, kc_ref, vct_ref, ovt_ref,
                     oc_ref, qt_ref, ks_ref, kw_ref, vst_ref, vwt_ref, kmax_scr, imp_scr, *, n_sel):
    i = pl.program_id(1)
    T = q_ref.shape[1]
    nc = kc_ref.shape[1]

    @pl.when(i == 0)
    def _():
        kmax_scr[...] = jnp.zeros_like(kmax_scr)

    cos, sin = cos_ref[...], sin_ref[...]
    lane = _iota((T, LANES), 1)
    row = _iota((LANES, T), 0)
    tq = i * T + _iota((1, T), 1)

    qn_t = _rms_row_groups(q_ref[0].T, HEAD_DIM) * qg_ref[...]
    qr_t = _rope_t(qn_t, cost_ref[...], sint_ref[...])
    zeros_half = jnp.zeros((HEAD_DIM, T), F32)

    def compressed_branch(rows):
        kc, vc_t = kc_ref[0, 0:rows, :], vct_ref[0, :, 0:rows]
        valid = _iota((rows, T), 0) * CMP_STRIDE + (CMP_LEN - 1) <= tq
        psum = jnp.zeros((rows, T), F32)
        o_heads = []
        for h in range(ATT_HEADS):
            qh = jnp.concatenate([qn_t[h * HEAD_DIM:(h + 1) * HEAD_DIM] * SCALE, zeros_half], axis=0).astype(BF16)
            s = jnp.where(valid, jnp.dot(kc, qh, preferred_element_type=F32), NEG)
            e = jnp.exp(s - jnp.max(s, axis=0, keepdims=True))
            p = jnp.where(valid, e / jnp.sum(e, axis=0, keepdims=True), 0.0)
            o_heads.append(jnp.dot(vc_t, p.astype(BF16), preferred_element_type=F32))
            psum = psum + p
        for j in range(ATT_HEADS // 2):
            oc_ref[0, :, j * LANES:(j + 1) * LANES] = _head_pair_rows(o_heads[2 * j], o_heads[2 * j + 1])
        imp_scr[...] = jnp.dot(ovt_ref[:, 0:rows], psum, precision=_HI, preferred_element_type=F32)

    n_slabs = max(nc // LANES, 1)
    last_done = jnp.maximum((i * T + T - CMP_LEN) // CMP_STRIDE, 0)
    needed = last_done // LANES + 1
    for k in range(1, n_slabs + 1):
        pl.when((needed == k) if k < n_slabs else (needed >= k))(
            functools.partial(compressed_branch, min(k * LANES, nc)))

    imp = imp_scr[...]
    blk_q = tq // SEL_BLOCK
    causal_blk = row <= blk_q
    forced = causal_blk & ((row == 0) | (row >= blk_q - 1))
    imp = jnp.where(forced, BIG, jnp.where(causal_blk, imp, NEG))
    imp = jnp.where(row < n_sel, imp, -jnp.inf)
    sel = _top_k_mask_t(imp, min(SEL_TOPK, n_sel))
    bias = jnp.where(sel > 0, 0.0, NEG)

    cos_k = jnp.where(lane < HEAD_DIM, cos[:, :LANES], 1.0)
    sin_k = jnp.where(lane < HEAD_DIM, sin[:, :LANES], 0.0)

    def key_pair(x, gain, idx):
        ms = jnp.sum(jnp.where(lane < HEAD_DIM, x * x, 0.0), axis=-1, keepdims=True) * (1.0 / HEAD_DIM)
        kn = _rope(x * lax.rsqrt(ms + NORM_EPS) * gain, cos_k, sin_k)
        kb = jnp.where(lane < HEAD_DIM, kn, 0.0).astype(BF16).astype(F32)
        return kb, _value_tile_t(x.T[HEAD_DIM:, :]), _running_max(kmax_scr, idx, _group_norms(kb, LANES)[0])

    ks, vst_ref[0, 0], ks_max = key_pair(kv_ref[0, :, LANES:2 * LANES], kg_ref[0:1, :], 0)
    kw, vwt_ref[0, 0], kw_max = key_pair(kv_ref[0, :, 2 * LANES:3 * LANES], kg_ref[1:2, :], 1)
    ks_ref[0] = ks.astype(BF16)
    kw_ref[0] = jnp.where(lane == HEAD_DIM, 1.0, kw).astype(BF16)

    q2_t = (qr_t * SCALE_LOG2).astype(BF16).astype(F32)
    row_h = _iota((HEAD_DIM, T), 0)
    for h, q_norm in enumerate(_row_group_norms(q2_t, HEAD_DIM)):
        qt_ref[0, h, 0:LANES, :] = (bias - q_norm * (ks_max * BOUND_SLACK)).astype(BF16)
        shift_rows = jnp.where(row_h == 0, -q_norm * (kw_max * BOUND_SLACK), 0.0)
        qt_ref[0, h, LANES:AUG_W, :] = jnp.concatenate([q2_t[h * HEAD_DIM:(h + 1) * HEAD_DIM], shift_rows],
                                                       axis=0).astype(BF16)


def _nsa_prep(p3, tables, qg_t, kg2, kc, vc_t, overlap_t, T=256):
    B, S, _ = p3.shape
    T = min(T, S)
    nc = kc.shape[1]
    n_sel = S // SEL_BLOCK
    tab = pl.BlockSpec((T, ATT_WIDTH), lambda b, i: (i, 0))
    tab_t = pl.BlockSpec((ATT_WIDTH, T), lambda b, i: (0, i))
    v_t = pl.BlockSpec((1, 1, VT_ROWS, T), lambda b, i: (b, i, 0, 0))
    v_t_shape = jax.ShapeDtypeStruct((B, S // T, VT_ROWS, T), BF16)
    return pl.pallas_call(
        functools.partial(_nsa_prep_kernel, n_sel=n_sel),
        grid=(B, S // T),
        in_specs=[pl.BlockSpec((1, T, ATT_WIDTH), lambda b, i: (b, i, COL_NQ // ATT_WIDTH)),
                  pl.BlockSpec((1, T, 3 * LANES), lambda b, i: (b, i, COL_NKV // (3 * LANES))),
                  tab, tab, tab_t, tab_t,
                  _const_spec((ATT_WIDTH, T)), _const_spec((2, LANES)),
                  pl.BlockSpec((1, nc, LANES), lambda b, i: (b, 0, 0)),
                  pl.BlockSpec((1, HEAD_DIM, nc), lambda b, i: (b, 0, 0)),
                  _const_spec((LANES, nc))],
        out_specs=[pl.BlockSpec((1, T, ATT_WIDTH), lambda b, i: (b, i, 0)),
                   pl.BlockSpec((1, ATT_HEADS, AUG_W, T), lambda b, i: (b, 0, 0, i)),
                   pl.BlockSpec((1, T, LANES), lambda b, i: (b, i, 0)),
                   pl.BlockSpec((1, T, LANES), lambda b, i: (b, i, 0)),
                   v_t, v_t],
        out_shape=[jax.ShapeDtypeStruct((B, S, ATT_WIDTH), F32),
                   jax.ShapeDtypeStruct((B, ATT_HEADS, AUG_W, S), BF16),
                   jax.ShapeDtypeStruct((B, S, LANES), BF16),
                   jax.ShapeDtypeStruct((B, S, LANES), BF16),
                   v_t_shape, v_t_shape],
        scratch_shapes=[pltpu.VMEM((2, SUBLANES, LANES), F32), pltpu.VMEM((LANES, T), F32)],
        compiler_params=_cparams(2),
        name="nsa_prep",
    )(p3, p3, *tables, qg_t, kg2, kc, vc_t, overlap_t)


def _out_ffn_kernel(x_ref, ym_ref, yb_ref, yn_ref, wo_ref, g_ref, w1_ref, w2_ref, o_ref, *, fc):
    mix = jnp.concatenate([ym_ref[...], yb_ref[...], yn_ref[...]], axis=1)
    x = x_ref[...] + jnp.dot(mix, wo_ref[...], preferred_element_type=F32)
    hb = (x * lax.rsqrt(jnp.mean(x * x, axis=-1, keepdims=True) + NORM_EPS) * g_ref[...]).astype(BF16)
    acc = x
    for c in range(w1_ref.shape[1] // fc):
        u = jnp.maximum(jnp.dot(hb, w1_ref[:, c * fc:(c + 1) * fc], preferred_element_type=F32), 0.0)
        acc = acc + jnp.dot((u * u).astype(BF16), w2_ref[c * fc:(c + 1) * fc, :], preferred_element_type=F32)
    o_ref[...] = acc


def _out_ffn(x2d, y_m, y_b, y_n, wo, g, w1, w2, tm=512, fc=1024):
    M, D = x2d.shape
    tm = min(tm, M)
    rows = lambda w: pl.BlockSpec((tm, w), lambda i: (i, 0))
    resident = lambda a: pl.BlockSpec(a.shape, lambda i: (0,) * a.ndim, pipeline_mode=pl.Buffered(1))
    return pl.pallas_call(
        functools.partial(_out_ffn_kernel, fc=fc),
        grid=(M // tm,),
        in_specs=[rows(D), rows(M_WIDTH), rows(ATT_WIDTH), rows(ATT_WIDTH),
                  resident(wo), _const_spec((1, D)), resident(w1), resident(w2)],
        out_specs=rows(D),
        out_shape=jax.ShapeDtypeStruct((M, D), F32),
        compiler_params=_cparams(1),
        name="out_ffn",
    )(x2d, y_m, y_b, y_n, wo, g, w1, w2)


def _rope_tables(S):
    inv_freq = jnp.exp(-math.log(ROPE_THETA) * jnp.arange(ROT_HALF, dtype=F32) * (2.0 / ROT_DIM))
    ang = jnp.arange(S, dtype=F32)[:, None] * inv_freq[None, :]
    cos, sin = jnp.cos(ang), jnp.sin(ang)
    rest = HEAD_DIM - ROT_DIM
    cos64 = jnp.concatenate([cos, cos, jnp.ones((S, rest), F32)], axis=1)
    sin64 = jnp.concatenate([-sin, sin, jnp.zeros((S, rest), F32)], axis=1)
    return jnp.tile(cos64, (1, ATT_HEADS)), jnp.tile(sin64, (1, ATT_HEADS))


def _overlap_matrix(nc):
    c_start = np.arange(nc)[:, None] * CMP_STRIDE
    s_start = np.arange(LANES)[None, :] * SEL_BLOCK
    return jnp.asarray(((c_start < s_start + SEL_BLOCK) & (c_start + CMP_LEN > s_start)).astype(np.float32))


def _compress_weights(pe, w1, w2):
    half = CMP_LEN // 2
    pe_r = jnp.concatenate([pe[0], pe[1]], axis=-1)
    pea = pe_r[:half].reshape(1, half * LANES)
    peb = pe_r[half:].reshape(1, half * LANES)
    w1r = w1.reshape(2, CMP_LEN, HEAD_DIM, CMP_HIDDEN)
    z = jnp.zeros_like(w1r[0])
    wk = jnp.concatenate([w1r[0], z], axis=-1)
    wv = jnp.concatenate([z, w1r[1]], axis=-1)
    wboth = jnp.concatenate([wk, wv], axis=1)
    w1a = wboth[:half].reshape(half * LANES, 2 * CMP_HIDDEN).astype(BF16)
    w1b = wboth[half:].reshape(half * LANES, 2 * CMP_HIDDEN).astype(BF16)
    z2 = jnp.zeros_like(w2[0])
    w2bd = jnp.concatenate([jnp.concatenate([w2[0], z2], axis=1),
                            jnp.concatenate([z2, w2[1]], axis=1)], axis=0).astype(BF16)
    return pea, peb, w1a, w1b, w2bd


def _pad_lanes(v, width=LANES):
    return jnp.concatenate([v, jnp.zeros((width - v.shape[0],), v.dtype)])[None, :]


def kernel(x, w_in, b_if, conv_qk, m_norm, moba_qk_norm, nsa_q_norm, nsa_k_norm, cmp_pe, cmp_w1, cmp_w2,
           w_out, norm_mix, norm_ffn, w_ff1, w_ff2):
    B, S, D = x.shape
    depth = w_in.shape[0]
    cos, sin = _rope_tables(S)
    tables = (cos, sin, cos.T, sin.T)
    overlap_t = _overlap_matrix(S // CMP_STRIDE).T
    x2d = x.reshape(B * S, D)
    for l in range(depth):
        p3 = _proj(x2d, norm_mix[l][None, :], w_in, l).reshape(B, S, P_W)

        y_m = _mlstm(p3, conv_qk[l], _pad_lanes(b_if[l]), m_norm[l][None, :])

        tile_g = lambda g: jnp.tile(g, ATT_HEADS)[None, :]
        rows_g = lambda g: jnp.broadcast_to(jnp.tile(g, ATT_HEADS)[:, None], (ATT_WIDTH, min(MOBA_BLOCK, S)))
        qt_b, kp_b, vt_b = _moba_prep(p3, tables, rows_g(moba_qk_norm[l, 0]), tile_g(moba_qk_norm[l, 1]))
        y_b = _flash(qt_b, kp_b, vt_b, BF16, MOBA_BLOCK)

        pea, peb, w1a, w1b, w2bd = _compress_weights(cmp_pe[l], cmp_w1[l], cmp_w2[l])
        kc, vc_t = _compress(p3, pea, peb, w1a, w1b, w2bd, _pad_lanes(nsa_k_norm[l, 0]))
        kg2 = jnp.concatenate([_pad_lanes(nsa_k_norm[l, 1]), _pad_lanes(nsa_k_norm[l, 2])], axis=0)
        o_c, qt_n, ks, kw, vs_t, vw_t = _nsa_prep(p3, tables, rows_g(nsa_q_norm[l]), kg2, kc, vc_t, overlap_t)
        o_s = _flash(qt_n, ks[:, None], vs_t[:, None], F32, SEL_BLOCK)
        y_n = _window_combine(qt_n, kw, vw_t, o_c, o_s, p3)

        x2d = _out_ffn(x2d, y_m.reshape(B * S, M_WIDTH), y_b.reshape(B * S, ATT_WIDTH), y_n.reshape(B * S, ATT_WIDTH),
                       w_out[l].astype(BF16), norm_ffn[l][None, :], w_ff1[l].astype(BF16), w_ff2[l].astype(BF16))
    return x2d.reshape(B, S, D)
```

```python
import functools
import math

import jax
import jax.numpy as jnp
import numpy as np
from jax import lax
from jax.experimental import pallas as pl
from jax.experimental.pallas import tpu as pltpu

F32 = jnp.float32
BF16 = jnp.bfloat16

HEAD_DIM = 64
M_HEADS = 4
M_HEAD_DIM = 128
M_WIDTH = M_HEADS * M_HEAD_DIM
CONV_W = 4
ATT_HEADS = 4
ATT_WIDTH = ATT_HEADS * HEAD_DIM
MOBA_BLOCK = 256
MOBA_TOPK = 3
CMP_LEN = 32
CMP_STRIDE = 16
CMP_HIDDEN = 128
SEL_BLOCK = 64
SEL_TOPK = 16
WINDOW = 512
ROPE_THETA = 500000.0
ROT_DIM = HEAD_DIM // 4
ROT_HALF = ROT_DIM // 2
NORM_EPS = 1e-6
NEG = -1e30
BIG = 1e9
SCALE = HEAD_DIM ** -0.5
SCALE_LOG2 = SCALE * math.log2(math.e)
BOUND_SLACK = 1.0 + 2.0 ** -7
L_TINY = 2.0 ** -100

SUBLANES = 8
LANES = 128
AUG_W = 2 * LANES
VT_ROWS = HEAD_DIM + 16
ATTEND_GROUP = 4
VMEM_LIMIT = 56 * 1024 * 1024

COL_MQK, COL_MV, COL_MO = 0, 1024, 1536
COL_BQ, COL_BK, COL_BV = 2048, 2304, 2560
COL_NQ, COL_NKV, COL_SM = 2816, 3072, 3456
P_W = 3584
SM_I, SM_F, SM_G = 0, 4, 8
_W_ATT0 = 4 * M_WIDTH + 2 * M_HEADS
_W_ATT1 = _W_ATT0 + 4 * ATT_WIDTH + 6 * HEAD_DIM
W_IN_SECTIONS = ((0, 4 * M_WIDTH), (_W_ATT0, _W_ATT1 - _W_ATT0), (4 * M_WIDTH, 2 * M_HEADS), (_W_ATT1, 3 * ATT_HEADS))

_NT = (((1,), (1,)), ((), ()))
_HI = lax.Precision.HIGHEST


def _iota(shape, dim):
    return lax.broadcasted_iota(jnp.int32, shape, dim)


def _cparams(n_axes):
    return pltpu.CompilerParams(dimension_semantics=("arbitrary",) * n_axes,
                                vmem_limit_bytes=VMEM_LIMIT)


def _const_spec(shape):
    nd = len(shape)
    return pl.BlockSpec(shape, lambda *_: (0,) * nd)


def _rms_groups(x, width):
    T, W = x.shape
    x2 = x * x
    lane = _iota((T, W), 1)
    scale = None
    for h in range(W // width):
        r = lax.rsqrt(jnp.mean(x2[:, h * width:(h + 1) * width], axis=-1, keepdims=True) + NORM_EPS)
        scale = r if scale is None else jnp.where(lane >= h * width, r, scale)
    return x * scale


def _rope(x, cos, sin):
    W = x.shape[1]
    lane = _iota(x.shape, 1) % HEAD_DIM
    up = pltpu.roll(x, W - ROT_HALF, 1)
    dn = pltpu.roll(x, ROT_HALF, 1)
    return x * cos + jnp.where(lane < ROT_HALF, up, dn) * sin


def _group_norms(x, width):
    x2 = x * x
    return [jnp.sqrt(jnp.sum(x2[:, h * width:(h + 1) * width], axis=-1, keepdims=True))
            for h in range(x.shape[1] // width)]


def _running_max(scr, idx, col):
    new = jnp.maximum(scr[idx][0:1, 0:1], jnp.max(col, axis=0, keepdims=True))
    scr[idx] = jnp.broadcast_to(new, scr.shape[1:])
    return new


def _value_tile_t(v_t):
    extra = jnp.where(_iota((VT_ROWS - HEAD_DIM, v_t.shape[1]), 0) == 0, 1.0, 0.0)
    return jnp.concatenate([v_t, extra], axis=0).astype(BF16)


def _rms_row_groups(x_t, width):
    parts = []
    for h in range(x_t.shape[0] // width):
        g = x_t[h * width:(h + 1) * width]
        parts.append(g * lax.rsqrt(jnp.mean(g * g, axis=0, keepdims=True) + NORM_EPS))
    return jnp.concatenate(parts, axis=0)


def _rope_t(x_t, cos_t, sin_t):
    W = x_t.shape[0]
    row = _iota(x_t.shape, 0) % HEAD_DIM
    up = pltpu.roll(x_t, W - ROT_HALF, 0)
    dn = pltpu.roll(x_t, ROT_HALF, 0)
    return x_t * cos_t + jnp.where(row < ROT_HALF, up, dn) * sin_t


def _row_group_norms(x_t, width):
    return [jnp.sqrt(jnp.sum(jnp.square(x_t[h * width:(h + 1) * width]), axis=0, keepdims=True))
            for h in range(x_t.shape[0] // width)]


def _top_k_mask_t(vals, k, rank_limit=None):
    row = _iota(vals.shape, 0)
    sel = jnp.zeros(vals.shape, jnp.int32)
    for r in range(k):
        mx = jnp.max(vals, axis=0, keepdims=True)
        idx = jnp.min(jnp.where(vals == mx, row, LANES), axis=0, keepdims=True)
        pick = row == idx
        mark = 1 if rank_limit is None else jnp.where(rank_limit > r, 1, 0)
        sel = jnp.where(pick, mark, sel)
        vals = jnp.where(pick, -jnp.inf, vals)
    return sel


def _proj_kernel(x_ref, g_ref, w_ref, o_ref, w_scr):
    @pl.when(pl.program_id(0) == 0)
    def _():
        dst = 0
        for src, width in W_IN_SECTIONS:
            w_scr[:, dst:dst + width] = w_ref[0, :, src:src + width].astype(BF16)
            dst += width
        w_scr[:, dst:] = jnp.zeros((w_scr.shape[0], w_scr.shape[1] - dst), BF16)

    x = x_ref[...]
    h = x * lax.rsqrt(jnp.mean(x * x, axis=-1, keepdims=True) + NORM_EPS) * g_ref[...]
    o_ref[...] = jnp.dot(h.astype(BF16), w_scr[...], preferred_element_type=F32)


def _proj(x2d, g, w_all, layer, tm=512):
    M, D = x2d.shape
    N = P_W
    assert w_all.shape[2] == sum(width for _, width in W_IN_SECTIONS)
    return pl.pallas_call(
        _proj_kernel,
        grid=(M // tm,),
        in_specs=[pl.BlockSpec((tm, D), lambda i: (i, 0)), _const_spec((1, D)),
                  pl.BlockSpec((1, D, w_all.shape[2]), lambda i: (layer, 0, 0), pipeline_mode=pl.Buffered(1))],
        out_specs=pl.BlockSpec((tm, N), lambda i: (i, 0)),
        out_shape=jax.ShapeDtypeStruct((M, N), F32),
        scratch_shapes=[pltpu.VMEM((D, N), BF16)],
        compiler_params=_cparams(1),
        name="in_proj",
    )(x2d, g, w_all)


def _mlstm_kernel(qk_ref, v_ref, o_ref, sm_ref, cw_ref, b_ref, g_ref, out_ref, xbuf, c_scr, m_scr, *, Lc):
    W2 = 2 * M_WIDTH
    D = M_HEAD_DIM

    @pl.when(pl.program_id(1) == 0)
    def _():
        xbuf[0:SUBLANES, :] = jnp.zeros((SUBLANES, W2), F32)
        c_scr[...] = jnp.zeros_like(c_scr)
        m_scr[...] = jnp.zeros_like(m_scr)

    x = qk_ref[0]
    xbuf[SUBLANES:SUBLANES + Lc, :] = x
    first = SUBLANES - CONV_W + 1
    y = cw_ref[0:1, :] * xbuf[first:first + Lc, :]
    for j in range(1, CONV_W):
        y = y + cw_ref[j:j + 1, :] * xbuf[first + j:first + j + Lc, :]
    xbuf[0:SUBLANES, :] = x[Lc - SUBLANES:Lc, :]
    qk = y * jax.nn.sigmoid(y)

    gi = sm_ref[0] + b_ref[...]
    lane = _iota((Lc, LANES), 1)
    lsig = jnp.minimum(gi, 0.0) - jnp.log1p(jnp.exp(-jnp.abs(gi)))
    gates = jnp.where(lane < SM_F, gi, jnp.where(lane < SM_F + M_HEADS, lsig, 0.0))
    gates_t = gates.T
    ri, ci = _iota((Lc, Lc), 0), _iota((Lc, Lc), 1)
    causal = ci <= ri
    b_col = jnp.dot(causal.astype(F32), gates, precision=_HI, preferred_element_type=F32)
    b_row = jnp.dot(gates_t[0:2 * M_HEADS, :], (ri <= ci).astype(F32), precision=_HI,
                    preferred_element_type=F32)
    ones_col = jnp.where(lane == 0, 1.0, 0.0)

    for h in range(M_HEADS):
        q = qk[:, h * D:(h + 1) * D]
        k = qk[:, M_WIDTH + h * D:M_WIDTH + (h + 1) * D] * (D ** -0.5)
        v_aug = jnp.concatenate([v_ref[0, :, h * D:(h + 1) * D], ones_col], axis=1).astype(BF16)
        bc = b_col[:, SM_F + h:SM_F + h + 1]
        br = b_row[SM_F + h:SM_F + h + 1, :]
        li_r = gates_t[SM_I + h:SM_I + h + 1, :]
        li_c = gates[:, SM_I + h:SM_I + h + 1]
        m_prev = m_scr[h][0:1, 0:1]

        dmat = jnp.where(causal, bc - br + li_r, -jnp.inf)
        inter = bc + m_prev
        m_t = jnp.maximum(inter, jnp.max(dmat, axis=-1, keepdims=True))
        w_intra = jnp.exp(dmat - m_t)
        w_prev = jnp.exp(inter - m_t)
        qb = q.astype(BF16)
        s = lax.dot_general(qb, k.astype(BF16), _NT, preferred_element_type=F32) * w_intra
        tot = (jnp.dot(s.astype(BF16), v_aug, preferred_element_type=F32)
               + w_prev * jnp.dot(qb, c_scr[h].astype(BF16), preferred_element_type=F32))
        den = tot[:, D:D + 1]
        hh = tot[:, :D] / jnp.maximum(jnp.abs(den), jnp.exp(-m_t))

        b_last = bc[Lc - 1:Lc, :]
        g_c = b_last - bc + li_c
        m_new = jnp.maximum(b_last + m_prev, jnp.max(g_c, axis=0, keepdims=True))
        a = jnp.exp(b_last + m_prev - m_new)
        kw_t = (k * jnp.exp(g_c - m_new)).T.astype(BF16)
        c_scr[h] = a * c_scr[h] + jnp.dot(kw_t, v_aug, preferred_element_type=F32)
        m_scr[h] = jnp.broadcast_to(m_new, (SUBLANES, LANES))

        hn = hh * lax.rsqrt(jnp.mean(hh * hh, axis=-1, keepdims=True) + NORM_EPS) * g_ref[0:1, h * D:(h + 1) * D]
        out_ref[0, :, h * D:(h + 1) * D] = (jax.nn.sigmoid(o_ref[0, :, h * D:(h + 1) * D]) * hn).astype(BF16)


def _mlstm(p3, conv_w, b_sm, m_norm, Lc=256):
    B, S, _ = p3.shape
    Lc = min(Lc, S)
    kern = functools.partial(_mlstm_kernel, Lc=Lc)
    return pl.pallas_call(
        kern,
        grid=(B, S // Lc),
        in_specs=[
            pl.BlockSpec((1, Lc, 2 * M_WIDTH), lambda b, c: (b, c, COL_MQK // (2 * M_WIDTH))),
            pl.BlockSpec((1, Lc, M_WIDTH), lambda b, c: (b, c, COL_MV // M_WIDTH)),
            pl.BlockSpec((1, Lc, M_WIDTH), lambda b, c: (b, c, COL_MO // M_WIDTH)),
            pl.BlockSpec((1, Lc, LANES), lambda b, c: (b, c, COL_SM // LANES)),
            _const_spec((CONV_W, 2 * M_WIDTH)),
            _const_spec((1, LANES)),
            _const_spec((1, M_WIDTH)),
        ],
        out_specs=pl.BlockSpec((1, Lc, M_WIDTH), lambda b, c: (b, c, 0)),
        out_shape=jax.ShapeDtypeStruct((B, S, M_WIDTH), BF16),
        scratch_shapes=[
            pltpu.VMEM((Lc + SUBLANES, 2 * M_WIDTH), F32),
            pltpu.VMEM((M_HEADS, M_HEAD_DIM, 2 * M_HEAD_DIM), F32),
            pltpu.VMEM((M_HEADS, SUBLANES, LANES), F32),
        ],
        compiler_params=_cparams(2),
        name="mlstm",
    )(p3, p3, p3, p3, conv_w, b_sm, m_norm)


def _moba_prep_kernel(q_ref, k_ref, v_ref, cos_ref, sin_ref, cost_ref, sint_ref, qg_ref, kg_ref,
                      qt_ref, kp_ref, vt_ref, kmean_scr, kmax_scr, *, n_blk):
    i = pl.program_id(1)
    T = q_ref.shape[1]

    @pl.when(i == 0)
    def _():
        kmean_scr[...] = jnp.zeros_like(kmean_scr)
        kmax_scr[...] = jnp.zeros_like(kmax_scr)

    kn = _rope(_rms_groups(k_ref[0], HEAD_DIM) * kg_ref[...], cos_ref[...], sin_ref[...])
    kb = kn.astype(BF16).astype(F32)
    k_norms = _group_norms(kb, HEAD_DIM)
    qn_t = _rope_t(_rms_row_groups(q_ref[0].T, HEAD_DIM) * qg_ref[...], cost_ref[...], sint_ref[...])
    q2_t = (qn_t * SCALE_LOG2).astype(BF16).astype(F32)
    q_norms = _row_group_norms(q2_t, HEAD_DIM)
    v_t = v_ref[0].T
    row_w = _iota((ATT_WIDTH, T), 0)
    row = _iota((n_blk, T), 0)
    zeros_half = jnp.zeros((HEAD_DIM, T), F32)
    kmeans = kmean_scr[0:n_blk, :]
    never = jnp.full((LANES - n_blk, T), NEG, F32).astype(BF16)
    for h in range(ATT_HEADS):
        qm = jnp.where((row_w >= h * HEAD_DIM) & (row_w < (h + 1) * HEAD_DIM), qn_t, 0.0)
        gs = jnp.dot(kmeans, qm, precision=_HI, preferred_element_type=F32)
        sel = _top_k_mask_t(jnp.where(row < i, gs, NEG), MOBA_TOPK, rank_limit=i)
        bound = q_norms[h] * (_running_max(kmax_scr, h, k_norms[h]) * BOUND_SLACK)
        qt_ref[0, h, 0:n_blk, :] = (jnp.where((sel > 0) | (row == i), 0.0, NEG) - bound).astype(BF16)
        if n_blk < LANES:
            qt_ref[0, h, n_blk:LANES, :] = never
        q_h = q2_t[h * HEAD_DIM:(h + 1) * HEAD_DIM]
        qt_ref[0, h, LANES:AUG_W, :] = jnp.concatenate([zeros_half, q_h] if h % 2 else [q_h, zeros_half],
                                                       axis=0).astype(BF16)
        vt_ref[0, h, 0] = _value_tile_t(v_t[h * HEAD_DIM:(h + 1) * HEAD_DIM, :])
    for j in range(ATT_HEADS // 2):
        kp_ref[0, j] = kb[:, j * LANES:(j + 1) * LANES].astype(BF16)
    kmean_scr[pl.ds(i, 1), :] = jnp.mean(kn, axis=0, keepdims=True)


def _moba_prep(p3, tables, qg_t, kg):
    B, S, _ = p3.shape
    T = MOBA_BLOCK
    blk = lambda col: pl.BlockSpec((1, T, ATT_WIDTH), lambda b, i: (b, i, col // ATT_WIDTH))
    tab = pl.BlockSpec((T, ATT_WIDTH), lambda b, i: (i, 0))
    tab_t = pl.BlockSpec((ATT_WIDTH, T), lambda b, i: (0, i))
    bf16_rows = 2 * SUBLANES
    n_blk = -(-(S // T) // bf16_rows) * bf16_rows
    assert n_blk <= LANES
    return pl.pallas_call(
        functools.partial(_moba_prep_kernel, n_blk=n_blk),
        grid=(B, S // T),
        in_specs=[blk(COL_BQ), blk(COL_BK), blk(COL_BV), tab, tab, tab_t, tab_t,
                  _const_spec((ATT_WIDTH, T)), _const_spec((1, ATT_WIDTH))],
        out_specs=[pl.BlockSpec((1, ATT_HEADS, AUG_W, T), lambda b, i: (b, 0, 0, i)),
                   pl.BlockSpec((1, ATT_HEADS // 2, T, LANES), lambda b, i: (b, 0, i, 0)),
                   pl.BlockSpec((1, ATT_HEADS, 1, VT_ROWS, T), lambda b, i: (b, 0, i, 0, 0))],
        out_shape=[jax.ShapeDtypeStruct((B, ATT_HEADS, AUG_W, S), BF16),
                   jax.ShapeDtypeStruct((B, ATT_HEADS // 2, S, LANES), BF16),
                   jax.ShapeDtypeStruct((B, ATT_HEADS, S // T, VT_ROWS, T), BF16)],
        scratch_shapes=[pltpu.VMEM((LANES, ATT_WIDTH), F32), pltpu.VMEM((ATT_HEADS, SUBLANES, LANES), F32)],
        compiler_params=_cparams(2),
        name="moba_prep",
    )(p3, p3, p3, *tables, qg_t, kg)


def _attend(tile_scores, tile_pv, lo, hi, acc_scr, m_scr, own_width=1):
    n_chain = acc_scr.shape[0]

    def run(exact):
        acc_scr[...] = jnp.zeros_like(acc_scr)
        if exact:
            m_scr[...] = jnp.full(m_scr.shape, NEG, F32)

        def tile(j, width, last):
            for c in range(n_chain):
                s_t = tile_scores(c, j, width, last)
                if exact:
                    m_prev = m_scr[c]
                    m_new = jnp.maximum(m_prev, jnp.max(s_t, axis=0, keepdims=True))
                    acc_scr[c] = (acc_scr[c] * jnp.exp2(m_prev - m_new)
                                  + tile_pv(c, j, width, jnp.exp2(s_t - m_new).astype(BF16)))
                    m_scr[c] = m_new
                else:
                    acc_scr[c] += tile_pv(c, j, width, jnp.exp2(s_t).astype(BF16))

        group = 1 if exact else ATTEND_GROUP
        n = hi - lo

        def body(i, carry):
            tile(lo + group * i, group, False)
            return carry

        lax.fori_loop(0, n // group, body, 0)
        start = lo + (n // group) * group
        merged = own_width if own_width < group else 0
        b = group // 2
        while b:
            if b != merged:
                take = (n & b) != 0

                @pl.when(take)
                def _(start=start, b=b):
                    tile(start, b, False)

                start = start + jnp.where(take, b, 0)
            b //= 2

        if merged:
            take = (n & merged) != 0

            @pl.when(take)
            def _():
                tile(hi - merged, merged + own_width, True)

            @pl.when(jnp.logical_not(take))
            def _():
                tile(hi, own_width, True)
        else:
            tile(hi, own_width, True)

    run(False)
    l_min = jnp.min(acc_scr[:, HEAD_DIM:HEAD_DIM + 1, :])

    @pl.when(jnp.logical_not(l_min >= L_TINY))
    def _():
        run(True)


def _head_pair_rows(o_a, o_b):
    return jnp.concatenate([o_a, o_b], axis=0).T


def _normalised_heads(acc_scr, T):
    heads = []
    for c in range(acc_scr.shape[0]):
        o = acc_scr[c, 0:HEAD_DIM, :] / acc_scr[c, HEAD_DIM:HEAD_DIM + 1, :]
        heads += [o[:, h * T:(h + 1) * T] for h in range(o.shape[1] // T)]
    return heads


def _flash_kernel(qt_ref, k_ref, vt_ref, o_ref, acc_scr, m_scr, *, TQ, mask_block):
    qi = pl.program_id(1)
    P, G, VH = qt_ref.shape[1], k_ref.shape[1], vt_ref.shape[1]
    TK = vt_ref.shape[4]
    own = TQ // TK
    hc = P // G
    N = hc * TQ
    q_cat = [jnp.concatenate([qt_ref[0, c * hc + h] for h in range(hc)], axis=1) for c in range(G)]
    def tile_scores(c, j, width, last):
        rows = width * TK
        lane, row = _iota((rows, LANES), 1), _iota((rows, LANES), 0)
        onehot = jnp.where(lane == (j * TK + row) // mask_block, 1.0, 0.0).astype(BF16)
        kt = jnp.concatenate([onehot, k_ref[0, c, pl.ds(pl.multiple_of(j * TK, TK), rows), :]], axis=1)
        s_t = jnp.dot(kt, q_cat[c], preferred_element_type=F32)
        if last:
            key = _iota((rows, N), 0) - (rows - TQ)
            s_t = jnp.where(key <= _iota((rows, N), 1) % TQ, s_t, NEG)
        return s_t

    def values_t(v, j, width):
        return jnp.concatenate([vt_ref[0, v, j + u] for u in range(width)], axis=1)

    def tile_pv(c, j, width, p):
        if VH == G:
            return jnp.dot(values_t(c, j, width), p, preferred_element_type=F32)
        return jnp.concatenate([jnp.dot(values_t(c * hc + h, j, width), p[:, h * TQ:(h + 1) * TQ],
                                        preferred_element_type=F32) for h in range(hc)], axis=1)

    _attend(tile_scores, tile_pv, 0, qi * own, acc_scr, m_scr, own_width=own)
    heads = _normalised_heads(acc_scr, TQ)
    for j in range(P // 2):
        o_ref[0, :, j * LANES:(j + 1) * LANES] = _head_pair_rows(heads[2 * j], heads[2 * j + 1]).astype(o_ref.dtype)


def _flash(q_t, k, v_t, out_dtype, mask_block, TQ=512):
    B, P, _, S = q_t.shape
    KH, VH, TK = k.shape[1], v_t.shape[1], v_t.shape[4]
    TQ = max(min(TQ, S), TK)
    assert v_t.shape == (B, VH, S // TK, VT_ROWS, TK) and k.shape == (B, KH, S, LANES) and TQ % TK == 0
    return pl.pallas_call(
        functools.partial(_flash_kernel, TQ=TQ, mask_block=mask_block),
        grid=(B, S // TQ),
        in_specs=[pl.BlockSpec((1, P, AUG_W, TQ), lambda b, i: (b, 0, 0, i)),
                  pl.BlockSpec((1, KH, S, LANES), lambda b, i: (b, 0, 0, 0)),
                  pl.BlockSpec((1, VH, S // TK, VT_ROWS, TK), lambda b, i: (b, 0, 0, 0, 0))],
        out_specs=pl.BlockSpec((1, TQ, P * HEAD_DIM), lambda b, i: (b, i, 0)),
        out_shape=jax.ShapeDtypeStruct((B, S, P * HEAD_DIM), out_dtype),
        scratch_shapes=[pltpu.VMEM((KH, VT_ROWS, P // KH * TQ), F32), pltpu.VMEM((KH, 1, P // KH * TQ), F32)],
        compiler_params=_cparams(2),
        name="flash",
    )(q_t, k, v_t)


def _window_kernel(qt_ref, k_ref, vt_ref, oc_ref, os_ref, sm_ref, y_ref, acc_scr, m_scr, *, T):
    R = ATT_HEADS
    TK = vt_ref.shape[3]
    own = T // TK
    qi = pl.program_id(1)
    q_cat = jnp.concatenate([qt_ref[0, h, LANES:AUG_W, :] for h in range(R)], axis=1)

    def tile_scores(c, j, width, last):
        rows = width * TK
        kt = k_ref[0, pl.ds(pl.multiple_of(j * TK, TK), rows), :]
        s_t = jnp.dot(kt, q_cat, preferred_element_type=F32)
        tq = qi * T + _iota((rows, R * T), 1) % T
        key = j * TK + _iota((rows, R * T), 0)
        return jnp.where((key <= tq) & (key > tq - WINDOW), s_t, NEG)

    def tile_pv(c, j, width, p):
        v_t = jnp.concatenate([vt_ref[0, j + u] for u in range(width)], axis=1)
        return jnp.dot(v_t, p, preferred_element_type=F32)

    first = jnp.maximum(qi * own - (WINDOW + TK - 1) // TK, 0)
    _attend(tile_scores, tile_pv, first, qi * own, acc_scr, m_scr, own_width=own)
    sig = jax.nn.sigmoid(sm_ref[0])
    lane = _iota((T, LANES), 1)
    heads = _normalised_heads(acc_scr, T)
    for j in range(R // 2):
        o_w = _head_pair_rows(heads[2 * j], heads[2 * j + 1])

        def gate(branch):
            c = SM_G + branch * R + 2 * j
            return jnp.where(lane < HEAD_DIM, sig[:, c:c + 1], sig[:, c + 1:c + 2])
        sl = slice(j * LANES, (j + 1) * LANES)
        y_ref[0, :, sl] = (gate(0) * oc_ref[0, :, sl] + gate(1) * os_ref[0, :, sl] + gate(2) * o_w).astype(y_ref.dtype)


def _window_combine(q_t, kw, vw_t, o_c, o_s, p3, T=512):
    B, R, _, S = q_t.shape
    TK = vw_t.shape[3]
    T = max(min(T, S), TK)
    packed = pl.BlockSpec((1, T, ATT_WIDTH), lambda b, i: (b, i, 0))
    return pl.pallas_call(
        functools.partial(_window_kernel, T=T),
        grid=(B, S // T),
        in_specs=[pl.BlockSpec((1, R, AUG_W, T), lambda b, i: (b, 0, 0, i)),
                  pl.BlockSpec((1, S, LANES), lambda b, i: (b, 0, 0)),
                  pl.BlockSpec((1, S // TK, VT_ROWS, TK), lambda b, i: (b, 0, 0, 0)),
                  packed, packed,
                  pl.BlockSpec((1, T, LANES), lambda b, i: (b, i, COL_SM // LANES))],
        out_specs=packed,
        out_shape=jax.ShapeDtypeStruct((B, S, ATT_WIDTH), BF16),
        scratch_shapes=[pltpu.VMEM((1, VT_ROWS, R * T), F32), pltpu.VMEM((1, 1, R * T), F32)],
        compiler_params=_cparams(2),
        name="nsa_window",
    )(q_t, kw, vw_t, o_c, o_s, p3)


def _compress_kernel(t_ref, pea_ref, peb_ref, w1a_ref, w1b_ref, w2_ref, kg_ref, kc_ref, vct_ref, t_scr):
    n = t_scr.shape[0]
    for r in range(CMP_STRIDE):
        t_scr[:, r * LANES:(r + 1) * LANES] = t_ref[0, pl.ds(r, n, stride=CMP_STRIDE), :]
    t = t_scr[...]
    a = jnp.dot((t + pea_ref[...]).astype(BF16), w1a_ref[...], preferred_element_type=F32)
    b = jnp.dot((t + peb_ref[...]).astype(BF16), w1b_ref[...], preferred_element_type=F32)
    hid = a + pltpu.roll(b, n - 1, 0)
    hid = hid * jax.nn.sigmoid(hid)
    kv = jnp.dot(hid.astype(BF16), w2_ref[...], preferred_element_type=F32)
    lane = _iota(kv.shape, 1)
    ms = jnp.sum(jnp.where(lane < HEAD_DIM, kv * kv, 0.0), axis=-1, keepdims=True) * (1.0 / HEAD_DIM)
    kc_ref[0] = jnp.where(lane < HEAD_DIM, kv * lax.rsqrt(ms + NORM_EPS) * kg_ref[...], 0.0).astype(kc_ref.dtype)
    vct_ref[0] = kv.T[HEAD_DIM:, :].astype(vct_ref.dtype)


def _compress(p3, pea, peb, w1a, w1b, w2, kg):
    B, S, _ = p3.shape
    n, W = S // CMP_STRIDE, CMP_STRIDE * LANES
    return pl.pallas_call(
        _compress_kernel,
        grid=(B,),
        in_specs=[pl.BlockSpec((1, S, LANES), lambda b: (b, 0, COL_NKV // LANES)),
                  _const_spec((1, W)), _const_spec((1, W)),
                  _const_spec(w1a.shape), _const_spec(w1b.shape), _const_spec(w2.shape),
                  _const_spec((1, LANES))],
        out_specs=[pl.BlockSpec((1, n, LANES), lambda b: (b, 0, 0)),
                   pl.BlockSpec((1, HEAD_DIM, n), lambda b: (b, 0, 0))],
        out_shape=[jax.ShapeDtypeStruct((B, n, LANES), BF16), jax.ShapeDtypeStruct((B, HEAD_DIM, n), BF16)],
        scratch_shapes=[pltpu.VMEM((n, W), F32)],
        compiler_params=_cparams(1),
        name="nsa_compress",
    )(p3, pea, peb, w1a, w1b, w2, kg)


def _nsa_prep_kernel(q_ref, kv_ref, cos_ref, sin_ref, cost_ref, sint_ref, qg_ref, kg_ref, kc_ref, vct_ref, ovt_ref,
                     oc_ref, qt_ref, ks_ref, kw_ref, vst_ref, vwt_ref, kmax_scr, imp_scr, *, n_sel):
    i = pl.program_id(1)
    T = q_ref.shape[1]
    nc = kc_ref.shape[1]

    @pl.when(i == 0)
    def _():
        kmax_scr[...] = jnp.zeros_like(kmax_scr)

    cos, sin = cos_ref[...], sin_ref[...]
    lane = _iota((T, LANES), 1)
    row = _iota((LANES, T), 0)
    tq = i * T + _iota((1, T), 1)

    qn_t = _rms_row_groups(q_ref[0].T, HEAD_DIM) * qg_ref[...]
    qr_t = _rope_t(qn_t, cost_ref[...], sint_ref[...])
    zeros_half = jnp.zeros((HEAD_DIM, T), F32)

    def compressed_branch(rows):
        kc, vc_t = kc_ref[0, 0:rows, :], vct_ref[0, :, 0:rows]
        valid = _iota((rows, T), 0) * CMP_STRIDE + (CMP_LEN - 1) <= tq
        psum = jnp.zeros((rows, T), F32)
        o_heads = []
        for h in range(ATT_HEADS):
            qh = jnp.concatenate([qn_t[h * HEAD_DIM:(h + 1) * HEAD_DIM] * SCALE, zeros_half], axis=0).astype(BF16)
            s = jnp.where(valid, jnp.dot(kc, qh, preferred_element_type=F32), NEG)
            e = jnp.exp(s - jnp.max(s, axis=0, keepdims=True))
            p = jnp.where(valid, e / jnp.sum(e, axis=0, keepdims=True), 0.0)
            o_heads.append(jnp.dot(vc_t, p.astype(BF16), preferred_element_type=F32))
            psum = psum + p
        for j in range(ATT_HEADS // 2):
            oc_ref[0, :, j * LANES:(j + 1) * LANES] = _head_pair_rows(o_heads[2 * j], o_heads[2 * j + 1])
        imp_scr[...] = jnp.dot(ovt_ref[:, 0:rows], psum, precision=_HI, preferred_element_type=F32)

    n_slabs = max(nc // LANES, 1)
    last_done = jnp.maximum((i * T + T - CMP_LEN) // CMP_STRIDE, 0)
    needed = last_done // LANES + 1
    for k in range(1, n_slabs + 1):
        pl.when((needed == k) if k < n_slabs else (needed >= k))(
            functools.partial(compressed_branch, min(k * LANES, nc)))

    imp = imp_scr[...]
    blk_q = tq // SEL_BLOCK
    causal_blk = row <= blk_q
    forced = causal_blk & ((row == 0) | (row >= blk_q - 1))
    imp = jnp.where(forced, BIG, jnp.where(causal_blk, imp, NEG))
    imp = jnp.where(row < n_sel, imp, -jnp.inf)
    sel = _top_k_mask_t(imp, min(SEL_TOPK, n_sel))
    bias = jnp.where(sel > 0, 0.0, NEG)

    cos_k = jnp.where(lane < HEAD_DIM, cos[:, :LANES], 1.0)
    sin_k = jnp.where(lane < HEAD_DIM, sin[:, :LANES], 0.0)

    def key_pair(x, gain, idx):
        ms = jnp.sum(jnp.where(lane < HEAD_DIM, x * x, 0.0), axis=-1, keepdims=True) * (1.0 / HEAD_DIM)
        kn = _rope(x * lax.rsqrt(ms + NORM_EPS) * gain, cos_k, sin_k)
        kb = jnp.where(lane < HEAD_DIM, kn, 0.0).astype(BF16).astype(F32)
        return kb, _value_tile_t(x.T[HEAD_DIM:, :]), _running_max(kmax_scr, idx, _group_norms(kb, LANES)[0])

    ks, vst_ref[0, 0], ks_max = key_pair(kv_ref[0, :, LANES:2 * LANES], kg_ref[0:1, :], 0)
    kw, vwt_ref[0, 0], kw_max = key_pair(kv_ref[0, :, 2 * LANES:3 * LANES], kg_ref[1:2, :], 1)
    ks_ref[0] = ks.astype(BF16)
    kw_ref[0] = jnp.where(lane == HEAD_DIM, 1.0, kw).astype(BF16)

    q2_t = (qr_t * SCALE_LOG2).astype(BF16).astype(F32)
    row_h = _iota((HEAD_DIM, T), 0)
    for h, q_norm in enumerate(_row_group_norms(q2_t, HEAD_DIM)):
        qt_ref[0, h, 0:LANES, :] = (bias - q_norm * (ks_max * BOUND_SLACK)).astype(BF16)
        shift_rows = jnp.where(row_h == 0, -q_norm * (kw_max * BOUND_SLACK), 0.0)
        qt_ref[0, h, LANES:AUG_W, :] = jnp.concatenate([q2_t[h * HEAD_DIM:(h + 1) * HEAD_DIM], shift_rows],
                                                       axis=0).astype(BF16)


def _nsa_prep(p3, tables, qg_t, kg2, kc, vc_t, overlap_t, T=256):
    B, S, _ = p3.shape
    T = min(T, S)
    nc = kc.shape[1]
    n_sel = S // SEL_BLOCK
    tab = pl.BlockSpec((T, ATT_WIDTH), lambda b, i: (i, 0))
    tab_t = pl.BlockSpec((ATT_WIDTH, T), lambda b, i: (0, i))
    v_t = pl.BlockSpec((1, 1, VT_ROWS, T), lambda b, i: (b, i, 0, 0))
    v_t_shape = jax.ShapeDtypeStruct((B, S // T, VT_ROWS, T), BF16)
    return pl.pallas_call(
        functools.partial(_nsa_prep_kernel, n_sel=n_sel),
        grid=(B, S // T),
        in_specs=[pl.BlockSpec((1, T, ATT_WIDTH), lambda b, i: (b, i, COL_NQ // ATT_WIDTH)),
                  pl.BlockSpec((1, T, 3 * LANES), lambda b, i: (b, i, COL_NKV // (3 * LANES))),
                  tab, tab, tab_t, tab_t,
                  _const_spec((ATT_WIDTH, T)), _const_spec((2, LANES)),
                  pl.BlockSpec((1, nc, LANES), lambda b, i: (b, 0, 0)),
                  pl.BlockSpec((1, HEAD_DIM, nc), lambda b, i: (b, 0, 0)),
                  _const_spec((LANES, nc))],
        out_specs=[pl.BlockSpec((1, T, ATT_WIDTH), lambda b, i: (b, i, 0)),
                   pl.BlockSpec((1, ATT_HEADS, AUG_W, T), lambda b, i: (b, 0, 0, i)),
                   pl.BlockSpec((1, T, LANES), lambda b, i: (b, i, 0)),
                   pl.BlockSpec((1, T, LANES), lambda b, i: (b, i, 0)),
                   v_t, v_t],
        out_shape=[jax.ShapeDtypeStruct((B, S, ATT_WIDTH), F32),
                   jax.ShapeDtypeStruct((B, ATT_HEADS, AUG_W, S), BF16),
                   jax.ShapeDtypeStruct((B, S, LANES), BF16),
                   jax.ShapeDtypeStruct((B, S, LANES), BF16),
                   v_t_shape, v_t_shape],
        scratch_shapes=[pltpu.VMEM((2, SUBLANES, LANES), F32), pltpu.VMEM((LANES, T), F32)],
        compiler_params=_cparams(2),
        name="nsa_prep",
    )(p3, p3, *tables, qg_t, kg2, kc, vc_t, overlap_t)


def _out_ffn_kernel(x_ref, ym_ref, yb_ref, yn_ref, wo_ref, g_ref, w1_ref, w2_ref, o_ref, *, fc):
    mix = jnp.concatenate([ym_ref[...], yb_ref[...], yn_ref[...]], axis=1)
    x = x_ref[...] + jnp.dot(mix, wo_ref[...], preferred_element_type=F32)
    hb = (x * lax.rsqrt(jnp.mean(x * x, axis=-1, keepdims=True) + NORM_EPS) * g_ref[...]).astype(BF16)
    acc = x
    for c in range(w1_ref.shape[1] // fc):
        u = jnp.maximum(jnp.dot(hb, w1_ref[:, c * fc:(c + 1) * fc], preferred_element_type=F32), 0.0)
        acc = acc + jnp.dot((u * u).astype(BF16), w2_ref[c * fc:(c + 1) * fc, :], preferred_element_type=F32)
    o_ref[...] = acc


def _out_ffn(x2d, y_m, y_b, y_n, wo, g, w1, w2, tm=512, fc=1024):
    M, D = x2d.shape
    tm = min(tm, M)
    rows = lambda w: pl.BlockSpec((tm, w), lambda i: (i, 0))
    resident = lambda a: pl.BlockSpec(a.shape, lambda i: (0,) * a.ndim, pipeline_mode=pl.Buffered(1))
    return pl.pallas_call(
        functools.partial(_out_ffn_kernel, fc=fc),
        grid=(M // tm,),
        in_specs=[rows(D), rows(M_WIDTH), rows(ATT_WIDTH), rows(ATT_WIDTH),
                  resident(wo), _const_spec((1, D)), resident(w1), resident(w2)],
        out_specs=rows(D),
        out_shape=jax.ShapeDtypeStruct((M, D), F32),
        compiler_params=_cparams(1),
        name="out_ffn",
    )(x2d, y_m, y_b, y_n, wo, g, w1, w2)


def _rope_tables(S):
    inv_freq = jnp.exp(-math.log(ROPE_THETA) * jnp.arange(ROT_HALF, dtype=F32) * (2.0 / ROT_DIM))
    ang = jnp.arange(S, dtype=F32)[:, None] * inv_freq[None, :]
    cos, sin = jnp.cos(ang), jnp.sin(ang)
    rest = HEAD_DIM - ROT_DIM
    cos64 = jnp.concatenate([cos, cos, jnp.ones((S, rest), F32)], axis=1)
    sin64 = jnp.concatenate([-sin, sin, jnp.zeros((S, rest), F32)], axis=1)
    return jnp.tile(cos64, (1, ATT_HEADS)), jnp.tile(sin64, (1, ATT_HEADS))


def _overlap_matrix(nc):
    c_start = np.arange(nc)[:, None] * CMP_STRIDE
    s_start = np.arange(LANES)[None, :] * SEL_BLOCK
    return jnp.asarray(((c_start < s_start + SEL_BLOCK) & (c_start + CMP_LEN > s_start)).astype(np.float32))


def _compress_weights(pe, w1, w2):
    half = CMP_LEN // 2
    pe_r = jnp.concatenate([pe[0], pe[1]], axis=-1)
    pea = pe_r[:half].reshape(1, half * LANES)
    peb = pe_r[half:].reshape(1, half * LANES)
    w1r = w1.reshape(2, CMP_LEN, HEAD_DIM, CMP_HIDDEN)
    z = jnp.zeros_like(w1r[0])
    wk = jnp.concatenate([w1r[0], z], axis=-1)
    wv = jnp.concatenate([z, w1r[1]], axis=-1)
    wboth = jnp.concatenate([wk, wv], axis=1)
    w1a = wboth[:half].reshape(half * LANES, 2 * CMP_HIDDEN).astype(BF16)
    w1b = wboth[half:].reshape(half * LANES, 2 * CMP_HIDDEN).astype(BF16)
    z2 = jnp.zeros_like(w2[0])
    w2bd = jnp.concatenate([jnp.concatenate([w2[0], z2], axis=1),
                            jnp.concatenate([z2, w2[1]], axis=1)], axis=0).astype(BF16)
    return pea, peb, w1a, w1b, w2bd


def _pad_lanes(v, width=LANES):
    return jnp.concatenate([v, jnp.zeros((width - v.shape[0],), v.dtype)])[None, :]


def kernel(x, w_in, b_if, conv_qk, m_norm, moba_qk_norm, nsa_q_norm, nsa_k_norm, cmp_pe, cmp_w1, cmp_w2,
           w_out, norm_mix, norm_ffn, w_ff1, w_ff2):
    B, S, D = x.shape
    depth = w_in.shape[0]
    cos, sin = _rope_tables(S)
    tables = (cos, sin, cos.T, sin.T)
    overlap_t = _overlap_matrix(S // CMP_STRIDE).T
    x2d = x.reshape(B * S, D)
    for l in range(depth):
        p3 = _proj(x2d, norm_mix[l][None, :], w_in, l).reshape(B, S, P_W)

        y_m = _mlstm(p3, conv_qk[l], _pad_lanes(b_if[l]), m_norm[l][None, :])

        tile_g = lambda g: jnp.tile(g, ATT_HEADS)[None, :]
        rows_g = lambda g: jnp.broadcast_to(jnp.tile(g, ATT_HEADS)[:, None], (ATT_WIDTH, min(MOBA_BLOCK, S)))
        qt_b, kp_b, vt_b = _moba_prep(p3, tables, rows_g(moba_qk_norm[l, 0]), tile_g(moba_qk_norm[l, 1]))
        y_b = _flash(qt_b, kp_b, vt_b, BF16, MOBA_BLOCK)

        pea, peb, w1a, w1b, w2bd = _compress_weights(cmp_pe[l], cmp_w1[l], cmp_w2[l])
        kc, vc_t = _compress(p3, pea, peb, w1a, w1b, w2bd, _pad_lanes(nsa_k_norm[l, 0]))
        kg2 = jnp.concatenate([_pad_lanes(nsa_k_norm[l, 1]), _pad_lanes(nsa_k_norm[l, 2])], axis=0)
        o_c, qt_n, ks, kw, vs_t, vw_t = _nsa_prep(p3, tables, rows_g(nsa_q_norm[l]), kg2, kc, vc_t, overlap_t)
        o_s = _flash(qt_n, ks[:, None], vs_t[:, None], F32, SEL_BLOCK)
        y_n = _window_combine(qt_n, kw, vw_t, o_c, o_s, p3)

        x2d = _out_ffn(x2d, y_m.reshape(B * S, M_WIDTH), y_b.reshape(B * S, ATT_WIDTH), y_n.reshape(B * S, ATT_WIDTH),
                       w_out[l].astype(BF16), norm_ffn[l][None, :], w_ff1[l].astype(BF16), w_ff2[l].astype(BF16))
    return x2d.reshape(B, S, D)
```
